```python
import math
import jax
import jax.numpy as jnp
from jax import lax
import numpy as np

D_MODEL = 2048
BATCH = 8
SEQ = 2048
DEPTH = 2
DEC_BATCH = 128
DEC_SEQ = 4
PAST_LEN = 2048
PAGE_SIZE = 128

EPS = 1e-6
N_BRANCH = 3
A_GROUPS = 8
A_GROUP_DIM = 128
A_WIDTH = A_GROUPS * A_GROUP_DIM
A_CHUNK = 128
B_HEADS = 8
B_DK = 128
B_DV = 128
B_QKV = B_HEADS * (2 * B_DK + B_DV)
B_WIDTH = B_HEADS * B_DV
CONV_W = 4
GDN_CHUNK = 64
C_HEADS = 8
C_DQ = 64
C_DV = 2 * C_DQ
C_QK = C_HEADS * 2 * C_DQ
C_WIDTH = C_HEADS * C_DV
Q_BLOCK = 128
N_BUCKETS = 32
MAX_DISTANCE = 128
N_EXPERTS = 64
TOP_K = 8
N_GROUPS = 8
TOPK_GROUPS = 4
D_EXPERT = 512
D_SHARED = 512
ROUTED_SCALE = 2.5
EXPERT_BLOCK = 128
IN_SIZES = (A_WIDTH, A_WIDTH, B_QKV, B_WIDTH, B_HEADS, B_HEADS, C_QK, C_QK, C_WIDTH, N_BRANCH * D_MODEL)
D_IN = sum(IN_SIZES)

kernel_name = 'hybrid_gmlp_gdn_diffattn_moe_step'


def rmsnorm(x, g):
    xf = x.astype(jnp.float32)
    y = xf * lax.rsqrt(jnp.mean(xf * xf, axis=-1, keepdims=True) + EPS)
    return (y * g.astype(jnp.float32)).astype(x.dtype)


def layernorm(x, g, b):
    xf = x.astype(jnp.float32)
    mu = jnp.mean(xf, axis=-1, keepdims=True)
    xc = xf - mu
    y = xc * lax.rsqrt(jnp.mean(xc * xc, axis=-1, keepdims=True) + EPS)
    return (y * g.astype(jnp.float32) + b.astype(jnp.float32)).astype(x.dtype)


def l2norm(x):
    xf = x.astype(jnp.float32)
    return xf * lax.rsqrt(jnp.sum(xf * xf, axis=-1, keepdims=True) + EPS)


def chunk_spatial_gate(u, v, sw, sb):
    bsz, L, _ = v.shape
    C = min(L, A_CHUNK)
    n = L // C
    causal = jnp.tril(jnp.ones((C, C), bool))
    w = jnp.where(causal[None], sw[:, :C, :C], 0.0)
    vb = v.reshape(bsz, n, C, A_GROUPS, A_GROUP_DIM)
    s = jnp.einsum('gts,bnsgc->bntgc', w.astype(v.dtype), vb)
    s = s + jnp.transpose(sb[:, :C])[None, None, :, :, None].astype(v.dtype)
    return u * s.reshape(bsz, L, A_WIDTH)


def causal_conv(x, prev, w):
    L = x.shape[1]
    xp = jnp.concatenate([prev.astype(x.dtype), x], axis=1)
    y = sum(xp[:, i:i + L] * w[i].astype(x.dtype) for i in range(CONV_W))
    return jax.nn.silu(y), xp[:, L:]


def gdn_chunked(q, k, v, g, beta, s0):
    f32 = jnp.float32
    bsz, L, H, _ = q.shape
    DV = v.shape[-1]
    C = math.gcd(L, GDN_CHUNK)
    n = L // C

    def blocks(t):
        return t.astype(f32).reshape(bsz, n, C, H, -1).transpose(1, 0, 3, 2, 4)

    q, k, v = blocks(q), blocks(k), blocks(v)
    g = blocks(g[..., None])[..., 0]
    beta = blocks(beta[..., None])[..., 0]
    gc = jnp.cumsum(g, axis=-1)
    causal = jnp.tril(jnp.ones((C, C), bool))
    strict = jnp.tril(jnp.ones((C, C), bool), -1)
    decay = jnp.exp(jnp.where(causal, gc[..., :, None] - gc[..., None, :], -jnp.inf))
    a = jnp.where(strict, jnp.einsum('nbhid,nbhjd->nbhij', k, k) * decay * beta[..., :, None], 0.0)
    tmat = a + jnp.eye(C, dtype=f32)
    u = lax.linalg.triangular_solve(tmat, v * beta[..., None], left_side=True, lower=True, unit_diagonal=True)
    w = lax.linalg.triangular_solve(tmat, k * (beta * jnp.exp(gc))[..., None], left_side=True, lower=True, unit_diagonal=True)
    qk = jnp.einsum('nbhid,nbhjd->nbhij', q, k) * decay

    def step(S, xs):
        q_i, k_i, u_i, w_i, gc_i, qk_i = xs
        v_new = u_i - jnp.einsum('bhcd,bhde->bhce', w_i, S)
        o = jnp.einsum('bhcd,bhde->bhce', q_i * jnp.exp(gc_i)[..., None], S) + jnp.einsum('bhij,bhje->bhie', qk_i, v_new)
        g_last = gc_i[..., -1:]
        S = S * jnp.exp(g_last)[..., None] + jnp.einsum('bhcd,bhce->bhde', k_i * jnp.exp(g_last - gc_i)[..., None], v_new)
        return S, o

    S, o = lax.scan(step, s0.astype(f32), (q, k, u, w, gc, qk))
    o = o.transpose(1, 0, 3, 2, 4).reshape(bsz, L, H, DV)
    return o, S


def rel_bias_lookup(qpos, kpos, table):
    n = jnp.maximum(qpos[:, None] - kpos[None, :], 0)
    exact = N_BUCKETS // 2
    nf = jnp.maximum(n, exact).astype(jnp.float32)
    large = exact + (jnp.log(nf / exact) / math.log(MAX_DISTANCE / exact) * (N_BUCKETS - exact)).astype(jnp.int32)
    bucket = jnp.where(n < exact, n, jnp.minimum(large, N_BUCKETS - 1))
    return jnp.transpose(table[bucket], (2, 0, 1)).astype(jnp.float32)


def diff_attend(q, segments, qpos, lam, rel_table):
    f32 = jnp.float32
    scale = C_DQ ** -0.5
    q1, q2 = q[..., :C_DQ], q[..., C_DQ:]
    logits1, logits2, lens = [], [], []
    for k, v, kpos in segments:
        bias = rel_bias_lookup(qpos, kpos, rel_table)[None]
        visible = (kpos[None, :] <= qpos[:, None])[None, None]
        s1 = jnp.einsum('bqhd,bkhd->bhqk', q1, k[..., :C_DQ], preferred_element_type=f32) * scale + bias
        s2 = jnp.einsum('bqhd,bkhd->bhqk', q2, k[..., C_DQ:], preferred_element_type=f32) * scale + bias
        logits1.append(jnp.where(visible, s1, -jnp.inf))
        logits2.append(jnp.where(visible, s2, -jnp.inf))
        lens.append(k.shape[1])
    p1 = jax.nn.softmax(jnp.concatenate(logits1, axis=-1), axis=-1)
    p2 = jax.nn.softmax(jnp.concatenate(logits2, axis=-1), axis=-1)
    probs = p1 - lam * p2
    offs = np.cumsum(lens)[:-1].tolist()
    parts = jnp.split(probs, offs, axis=-1)
    return sum(jnp.einsum('bhqk,bkhd->bqhd', pr.astype(seg[1].dtype), seg[1], preferred_element_type=f32)
               for pr, seg in zip(parts, segments))


def diff_attn_prompt(q, k, v, lam, rel_table):
    bsz, S = q.shape[:2]
    kpos = jnp.arange(S, dtype=jnp.int32)

    def block(i):
        qi = lax.dynamic_slice_in_dim(q, i * Q_BLOCK, Q_BLOCK, axis=1)
        qpos = i * Q_BLOCK + jnp.arange(Q_BLOCK, dtype=jnp.int32)
        return diff_attend(qi, [(k, v, kpos)], qpos, lam, rel_table)

    o = lax.map(block, jnp.arange(S // Q_BLOCK, dtype=jnp.int32))
    return jnp.moveaxis(o, 0, 1).reshape(bsz, S, C_HEADS, C_DV)


def routed_experts(xt, top_e, wts, w_gate, w_up, w_down):
    f32 = jnp.float32
    T = xt.shape[0]
    n_assign = T * TOP_K
    n_blocks = -(-(n_assign + N_EXPERTS * (EXPERT_BLOCK - 1)) // EXPERT_BLOCK)
    e_flat = top_e.reshape(-1)
    order = jnp.argsort(e_flat)
    e_sorted = e_flat[order]
    tok_sorted = (order // TOP_K).astype(jnp.int32)
    w_sorted = wts.reshape(-1)[order]
    counts = jnp.bincount(e_flat, length=N_EXPERTS)
    padded = (counts + EXPERT_BLOCK - 1) // EXPERT_BLOCK * EXPERT_BLOCK
    start = jnp.cumsum(counts) - counts
    pend = jnp.cumsum(padded)
    pstart = pend - padded
    dest = pstart[e_sorted] + jnp.arange(n_assign, dtype=jnp.int32) - start[e_sorted]
    n_rows = n_blocks * EXPERT_BLOCK
    tok_buf = jnp.zeros((n_rows,), jnp.int32).at[dest].set(tok_sorted)
    w_buf = jnp.zeros((n_rows,), f32).at[dest].set(w_sorted)
    blk_e = jnp.minimum(jnp.searchsorted(pend, jnp.arange(n_blocks, dtype=jnp.int32) * EXPERT_BLOCK, side='right'), N_EXPERTS - 1)

    def step(acc, xs):
        tok, wt, e = xs
        xb = xt[tok]
        hb = jax.nn.silu(xb @ w_gate[e]) * (xb @ w_up[e])
        yb = jnp.matmul(hb, w_down[e], preferred_element_type=f32) * wt[:, None]
        return acc.at[tok].add(yb), None

    acc, _ = lax.scan(step, jnp.zeros(xt.shape, f32),
                      (tok_buf.reshape(n_blocks, EXPERT_BLOCK), w_buf.reshape(n_blocks, EXPERT_BLOCK), blk_e))
    return acc


def moe_ffn(h, w_router, router_bias, w_gate, w_up, w_down, ws_gate, ws_up, ws_down):
    f32 = jnp.float32
    shape = h.shape
    xt = h.reshape(-1, shape[-1])
    T = xt.shape[0]
    scores = jax.nn.sigmoid(jnp.matmul(xt, w_router, preferred_element_type=f32))
    sel = scores + router_bias.astype(f32)
    grp = sel.reshape(T, N_GROUPS, N_EXPERTS // N_GROUPS)
    grp_score = jnp.sum(lax.top_k(grp, 2)[0], axis=-1)
    _, top_g = lax.top_k(grp_score, TOPK_GROUPS)
    gmask = jnp.any(top_g[..., None] == jnp.arange(N_GROUPS), axis=1)
    emask = jnp.repeat(gmask, N_EXPERTS // N_GROUPS, axis=1)
    _, top_e = lax.top_k(jnp.where(emask, sel, -jnp.inf), TOP_K)
    wts = jnp.take_along_axis(scores, top_e, axis=1)
    wts = wts / jnp.sum(wts, axis=-1, keepdims=True) * ROUTED_SCALE
    routed = routed_experts(xt, top_e, wts, w_gate, w_up, w_down)
    shared = (jax.nn.silu(xt @ ws_gate) * (xt @ ws_up)) @ ws_down
    return (routed + shared.astype(f32)).astype(h.dtype).reshape(shape)


def _trunk(x, c, p, past):
    f32 = jnp.float32
    bsz, L, _ = x.shape
    is_sample = past is not None
    pos = (PAST_LEN if is_sample else 0) + jnp.arange(L, dtype=jnp.int32)
    splits = np.cumsum(IN_SIZES)[:-1].tolist()
    new_k, new_v, new_s, new_conv, new_va = [], [], [], [], []
    for l in range(DEPTH):
        mod = (jax.nn.silu(c.astype(f32)) @ p['w_mod'][l].astype(f32) + p['b_mod'][l].astype(f32)).astype(x.dtype)
        sh1, sc1, g1, sh2, sc2, g2 = [m[:, None, :] for m in jnp.split(mod, 6, axis=-1)]

        h = rmsnorm(x, p['norm_mix_g'][l]) * (1 + sc1) + sh1
        proj = h @ p['w_in'][l]
        u_a, v_a, qkv_b, z_b, a_b, b_b, q_c, k_c, v_c, gate_logits = jnp.split(proj, splits, axis=-1)

        v_a = layernorm(jax.nn.gelu(v_a), p['ln_v_g'][l], p['ln_v_b'][l])
        y_a = chunk_spatial_gate(jax.nn.gelu(u_a), v_a, p['spatial_w'][l], p['spatial_b'][l])

        if is_sample:
            conv_prev = past['state_conv'][l]
            s0 = past['state_gdn'][l]
        else:
            conv_prev = jnp.zeros((bsz, CONV_W - 1, B_QKV), x.dtype)
            s0 = jnp.zeros((bsz, B_HEADS, B_DK, B_DV), f32)
        qkv_b, conv_state = causal_conv(qkv_b, conv_prev, p['conv_w'][l])
        q_b, k_b, v_b = jnp.split(qkv_b, [B_HEADS * B_DK, 2 * B_HEADS * B_DK], axis=-1)
        q_b = l2norm(q_b.reshape(bsz, L, B_HEADS, B_DK)) * (B_DK ** -0.5)
        k_b = l2norm(k_b.reshape(bsz, L, B_HEADS, B_DK))
        v_b = v_b.reshape(bsz, L, B_HEADS, B_DV)
        beta = jax.nn.sigmoid(b_b.astype(f32))
        g_log = -jnp.exp(p['A_log'][l].astype(f32)) * jax.nn.softplus(a_b.astype(f32) + p['dt_bias'][l].astype(f32))
        o_b, s_new = gdn_chunked(q_b, k_b, v_b, g_log, beta, s0)
        z_b = jax.nn.silu(z_b.astype(f32)).reshape(bsz, L, B_HEADS, B_DV)
        y_b = (rmsnorm(o_b, p['gdn_norm_g'][l]) * z_b).reshape(bsz, L, B_WIDTH).astype(x.dtype)

        q_c = q_c.reshape(bsz, L, C_HEADS, 2 * C_DQ)
        k_c = k_c.reshape(bsz, L, C_HEADS, 2 * C_DQ)
        v_c = v_c.reshape(bsz, L, C_HEADS, C_DV)
        lam_init = 0.8 - 0.6 * math.exp(-0.3 * l)
        lam = (jnp.exp(jnp.sum(p['lambda_q1'][l].astype(f32) * p['lambda_k1'][l].astype(f32)))
               - jnp.exp(jnp.sum(p['lambda_q2'][l].astype(f32) * p['lambda_k2'][l].astype(f32))) + lam_init)
        if is_sample:
            page_table = past['page_table']
            past_len = page_table.shape[1] * PAGE_SIZE
            k_past = past['cache_k'][l, page_table].reshape(bsz, past_len, C_HEADS, 2 * C_DQ)
            v_past = past['cache_v'][l, page_table].reshape(bsz, past_len, C_HEADS, C_DV)
            kpos_past = jnp.arange(past_len, dtype=jnp.int32)
            o_c = diff_attend(q_c, [(k_past, v_past, kpos_past), (k_c, v_c, pos)], pos, lam, p['rel_bias'])
        else:
            o_c = diff_attn_prompt(q_c, k_c, v_c, lam, p['rel_bias'])
        y_c = (rmsnorm(o_c, p['subln_g'][l]) * (1.0 - lam_init)).reshape(bsz, L, C_WIDTH).astype(x.dtype)

        gates = jax.nn.sigmoid(gate_logits.astype(f32)).reshape(bsz, L, N_BRANCH, D_MODEL).astype(x.dtype)
        merged = sum(gates[:, :, n] * (y_n @ p['w_branch'][l, n]) for n, y_n in enumerate((y_a, y_b, y_c)))
        x = x + g1 * (merged @ p['w_out'][l])

        h = rmsnorm(x, p['norm_moe_g'][l]) * (1 + sc2) + sh2
        x = x + g2 * moe_ffn(h, p['w_router'][l], p['router_bias'][l], p['w_gate'][l], p['w_up'][l], p['w_down'][l],
                             p['ws_gate'][l], p['ws_up'][l], p['ws_down'][l])

        new_k.append(k_c)
        new_v.append(v_c)
        new_s.append(s_new.astype(x.dtype))
        new_conv.append(conv_state)
        if is_sample:
            new_va.append(v_a)
    y = rmsnorm(x, p['final_norm_g'])
    chunk_v = jnp.stack(new_va) if is_sample else None
    return y, jnp.stack(new_k), jnp.stack(new_v), jnp.stack(new_s), jnp.stack(new_conv), chunk_v


def setup_inputs(seed: int = 0) -> dict:
    key = jax.random.key(seed)
    keys = iter(jax.random.split(key, 48))

    def normal(shape, scale):
        return jax.random.normal(next(keys), shape, jnp.float32) * scale

    n_pages = PAST_LEN // PAGE_SIZE
    n_pool = (DEC_BATCH * n_pages * 5 + 3) // 4
    D = D_MODEL
    inp = {}
    inp['x_prompt'] = normal((BATCH, SEQ, D), 1.0)
    inp['x_sample'] = normal((DEC_BATCH, DEC_SEQ, D), 1.0)
    inp['cache_k'] = normal((DEPTH, n_pool, PAGE_SIZE, C_HEADS, 2 * C_DQ), 1.0)
    inp['cache_v'] = normal((DEPTH, n_pool, PAGE_SIZE, C_HEADS, C_DV), 1.0)
    inp['state_gdn'] = normal((DEPTH, DEC_BATCH, B_HEADS, B_DK, B_DV), 0.1)
    inp['state_conv'] = normal((DEPTH, DEC_BATCH, CONV_W - 1, B_QKV), 1.0)
    perm = jax.random.permutation(next(keys), n_pool)
    inp['page_table'] = perm[:DEC_BATCH * n_pages].reshape(DEC_BATCH, n_pages).astype(jnp.int32)
    inp['c_prompt'] = normal((BATCH, D), 1.0)
    inp['c_sample'] = normal((DEC_BATCH, D), 1.0)
    inp['w_mod'] = normal((DEPTH, D, 6 * D), 0.5 * D ** -0.5)
    inp['b_mod'] = normal((DEPTH, 6 * D), 0.02)
    inp['norm_mix_g'] = 1.0 + normal((DEPTH, D), 0.02)
    inp['w_in'] = normal((DEPTH, D, D_IN), D ** -0.5)
    inp['spatial_w'] = normal((DEPTH, A_GROUPS, A_CHUNK, A_CHUNK), A_CHUNK ** -0.5)
    inp['spatial_b'] = 1.0 + normal((DEPTH, A_GROUPS, A_CHUNK), 0.1)
    inp['ln_v_g'] = 1.0 + normal((DEPTH, A_WIDTH), 0.02)
    inp['ln_v_b'] = normal((DEPTH, A_WIDTH), 0.02)
    inp['conv_w'] = normal((DEPTH, CONV_W, B_QKV), CONV_W ** -0.5)
    inp['A_log'] = jnp.log(jax.random.uniform(next(keys), (DEPTH, B_HEADS), jnp.float32, 1.0, 16.0))
    dt = jnp.exp(jax.random.uniform(next(keys), (DEPTH, B_HEADS), jnp.float32, math.log(1e-3), math.log(1e-1)))
    inp['dt_bias'] = dt + jnp.log(-jnp.expm1(-dt))
    inp['gdn_norm_g'] = 1.0 + normal((DEPTH, B_DV), 0.02)
    inp['lambda_q1'] = normal((DEPTH, C_DQ), 0.1)
    inp['lambda_k1'] = normal((DEPTH, C_DQ), 0.1)
    inp['lambda_q2'] = normal((DEPTH, C_DQ), 0.1)
    inp['lambda_k2'] = normal((DEPTH, C_DQ), 0.1)
    inp['subln_g'] = 1.0 + normal((DEPTH, C_DV), 0.02)
    inp['rel_bias'] = normal((N_BUCKETS, C_HEADS), 0.5)
    inp['w_branch'] = normal((DEPTH, N_BRANCH, A_WIDTH, D), A_WIDTH ** -0.5)
    inp['w_out'] = normal((DEPTH, D, D), D ** -0.5)
    inp['norm_moe_g'] = 1.0 + normal((DEPTH, D), 0.02)
    inp['w_router'] = normal((DEPTH, D, N_EXPERTS), D ** -0.5)
    inp['router_bias'] = normal((DEPTH, N_EXPERTS), 0.01)
    inp['w_gate'] = normal((DEPTH, N_EXPERTS, D, D_EXPERT), D ** -0.5)
    inp['w_up'] = normal((DEPTH, N_EXPERTS, D, D_EXPERT), D ** -0.5)
    inp['w_down'] = normal((DEPTH, N_EXPERTS, D_EXPERT, D), D_EXPERT ** -0.5)
    inp['ws_gate'] = normal((DEPTH, D, D_SHARED), D ** -0.5)
    inp['ws_up'] = normal((DEPTH, D, D_SHARED), D ** -0.5)
    inp['ws_down'] = normal((DEPTH, D_SHARED, D), D_SHARED ** -0.5)
    inp['final_norm_g'] = 1.0 + normal((D,), 0.02)
    return inp


def reference(x_prompt, x_sample, cache_k, cache_v, state_gdn, state_conv, page_table, c_prompt, c_sample,
              w_mod, b_mod, norm_mix_g, w_in, spatial_w, spatial_b, ln_v_g, ln_v_b, conv_w, A_log, dt_bias,
              gdn_norm_g, lambda_q1, lambda_k1, lambda_q2, lambda_k2, subln_g, rel_bias, w_branch, w_out,
              norm_moe_g, w_router, router_bias, w_gate, w_up, w_down, ws_gate, ws_up, ws_down, final_norm_g):
    p = dict(w_mod=w_mod, b_mod=b_mod, norm_mix_g=norm_mix_g, w_in=w_in, spatial_w=spatial_w, spatial_b=spatial_b,
             ln_v_g=ln_v_g, ln_v_b=ln_v_b, conv_w=conv_w, A_log=A_log, dt_bias=dt_bias, gdn_norm_g=gdn_norm_g,
             lambda_q1=lambda_q1, lambda_k1=lambda_k1, lambda_q2=lambda_q2, lambda_k2=lambda_k2, subln_g=subln_g,
             rel_bias=rel_bias, w_branch=w_branch, w_out=w_out, norm_moe_g=norm_moe_g, w_router=w_router,
             router_bias=router_bias, w_gate=w_gate, w_up=w_up, w_down=w_down, ws_gate=ws_gate, ws_up=ws_up,
             ws_down=ws_down, final_norm_g=final_norm_g)
    past = dict(cache_k=cache_k, cache_v=cache_v, state_gdn=state_gdn, state_conv=state_conv, page_table=page_table)
    y_prompt, k_prompt, v_prompt, gdn_prompt, conv_prompt, _ = _trunk(x_prompt, c_prompt, p, None)
    y_sample, k_sample, v_sample, gdn_sample, conv_sample, chunk_v_sample = _trunk(x_sample, c_sample, p, past)
    return (y_prompt, y_sample, k_prompt, v_prompt, k_sample, v_sample, gdn_prompt, gdn_sample,
            conv_prompt, conv_sample, chunk_v_sample)
```

```python
import functools
import math

import numpy as np
import jax
import jax.numpy as jnp
from jax import lax
from jax.experimental import pallas as pl
from jax.experimental.pallas import tpu as pltpu

F32 = jnp.float32
BF16 = jnp.bfloat16
EPS = 1e-6

LANE = 128
SUBLANE = 8
VMEM_LIMIT_BYTES = 56 * 1024 * 1024

A_GROUPS = 8
A_CHUNK = 128
B_HEADS = 8
B_DK = 128
CONV_W = 4
GDN_CHUNK = 64
C_HEADS = 8
C_DQ = 64
N_BUCKETS = 32
MAX_DISTANCE = 128
TOP_K = 8
N_GROUPS = 8
TOPK_GROUPS = 4
ROUTED_SCALE = 2.5
N_BRANCH = 3
GELU_C = float(np.float32(np.sqrt(2.0 / np.pi)))


def _cparams(sem):
    return pltpu.CompilerParams(dimension_semantics=sem, vmem_limit_bytes=VMEM_LIMIT_BYTES)


def _sigmoid(x):
    return 1.0 / (1.0 + jnp.exp(-x))


def _silu(x):
    return x * _sigmoid(x)


def _gelu(x):
    return x * (0.5 * (1.0 + jnp.tanh(GELU_C * (x + 0.044715 * (x * x * x)))))


def _dot(a, b):
    return jnp.dot(a.astype(BF16), b.astype(BF16), preferred_element_type=F32)


def _dot_nt(a, b):
    return lax.dot_general(a.astype(BF16), b.astype(BF16), (((1,), (1,)), ((), ())),
                           preferred_element_type=F32)


def _dot_tn(a, b):
    return lax.dot_general(a.astype(BF16), b.astype(BF16), (((0,), (0,)), ((), ())),
                           preferred_element_type=F32)


def _split2(a):
    hi = a.astype(BF16)
    return hi, (a - hi.astype(F32)).astype(BF16)


def _dot_hi(a, b):
    a0, a1 = _split2(a)
    b0, b1 = _split2(b)
    d = lambda x, y: jnp.dot(x, y, preferred_element_type=F32)
    return (d(a1, b0) + d(a0, b1)) + d(a0, b0)


def _rms(x):
    return x * lax.rsqrt(jnp.mean(x * x, axis=-1, keepdims=True) + EPS)


def _mod_kernel(c_ref, w_ref, b_ref, o_ref):
    o_ref[...] = _dot(_silu(c_ref[...]), w_ref[...]) + b_ref[...]


def _mod_call(c_all, w_mod, b_mod):
    depth, d, n = w_mod.shape
    m = c_all.shape[0]
    tn = 1536 if n % 1536 == 0 else n
    return pl.pallas_call(
        _mod_kernel,
        grid=(depth, n // tn),
        in_specs=[pl.BlockSpec((m, d), lambda l, j: (0, 0)),
                  pl.BlockSpec((None, d, tn), lambda l, j: (l, 0, j)),
                  pl.BlockSpec((None, 1, tn), lambda l, j: (l, 0, j))],
        out_specs=pl.BlockSpec((None, m, tn), lambda l, j: (l, 0, j)),
        out_shape=jax.ShapeDtypeStruct((depth, m, n), F32),
        compiler_params=_cparams(("parallel", "parallel")),
        name="mod",
    )(c_all, w_mod, b_mod.reshape(depth, 1, n))


class _Group:
    def __init__(self, mod_arr, per_token, seq, tm):
        self.mod_arr = mod_arr
        self.per_token = per_token
        self.seq = seq
        self.tm = tm

    def mod_spec(self, layer, chunk, d):
        tm, seq = self.tm, self.seq
        if self.per_token:
            return pl.BlockSpec((None, tm, d), lambda i, *_: (layer, i, chunk))
        return pl.BlockSpec((None, None, 1, d), lambda i, *_: (layer, (i * tm) // seq, 0, chunk))


def _inproj_kernel(x_ref, sc_ref, sh_ref, g_ref, w_ref, o_ref, h_scr):
    @pl.when(pl.program_id(1) == 0)
    def _():
        h = _rms(x_ref[...]) * g_ref[...]
        h_scr[...] = (h * (1.0 + sc_ref[...]) + sh_ref[...]).astype(BF16)

    o_ref[...] = jnp.dot(h_scr[...], w_ref[...].astype(BF16), preferred_element_type=F32)


def _inproj_call(x, grp, layer, chunks, g, w, tn):
    t, d = x.shape
    n = w.shape[1]
    tm = grp.tm
    return pl.pallas_call(
        _inproj_kernel,
        grid=(t // tm, n // tn),
        in_specs=[pl.BlockSpec((tm, d), lambda i, j: (i, 0)),
                  grp.mod_spec(layer, chunks[0], d),
                  grp.mod_spec(layer, chunks[1], d),
                  pl.BlockSpec((1, d), lambda i, j: (0, 0)),
                  pl.BlockSpec((d, tn), lambda i, j: (0, j))],
        out_specs=pl.BlockSpec((tm, tn), lambda i, j: (i, j)),
        out_shape=jax.ShapeDtypeStruct((t, n), F32),
        scratch_shapes=[pltpu.VMEM((tm, d), BF16)],
        compiler_params=_cparams(("parallel", "arbitrary")),
        name="inproj",
    )(x, grp.mod_arr, grp.mod_arr, g.reshape(1, d), w)


def _mixa_kernel(u_ref, v_ref, w_ref, sb_ref, lg_ref, lb_ref, y_ref, vn_ref):
    v = _gelu(v_ref[...])
    xc = v - jnp.mean(v, axis=-1, keepdims=True)
    vn = xc * lax.rsqrt(jnp.mean(xc * xc, axis=-1, keepdims=True) + EPS) * lg_ref[...] + lb_ref[...]
    vn_ref[...] = vn
    vb = vn.astype(BF16)
    gd = LANE
    for g in range(A_GROUPS):
        sl = slice(g * gd, (g + 1) * gd)
        s = jnp.dot(w_ref[g].astype(BF16), vb[:, sl], preferred_element_type=F32) + sb_ref[:, sl]
        y_ref[:, sl] = (_gelu(u_ref[:, sl]) * s).astype(y_ref.dtype)


def _mixa_call(proj, w_chunk, sb_full, ln_g, ln_b):
    t = proj.shape[0]
    wdt = A_GROUPS * LANE
    c = A_CHUNK
    return pl.pallas_call(
        _mixa_kernel,
        grid=(t // c,),
        in_specs=[pl.BlockSpec((c, wdt), lambda i: (i, 0)),
                  pl.BlockSpec((c, wdt), lambda i: (i, 1)),
                  pl.BlockSpec((A_GROUPS, c, c), lambda i: (0, 0, 0)),
                  pl.BlockSpec((c, wdt), lambda i: (0, 0)),
                  pl.BlockSpec((1, wdt), lambda i: (0, 0)),
                  pl.BlockSpec((1, wdt), lambda i: (0, 0))],
        out_specs=[pl.BlockSpec((c, wdt), lambda i: (i, 0)),
                   pl.BlockSpec((c, wdt), lambda i: (i, 0))],
        out_shape=[jax.ShapeDtypeStruct((t, wdt), BF16),
                   jax.ShapeDtypeStruct((t, wdt), F32)],
        compiler_params=_cparams(("parallel",)),
        name="mixer_a",
    )(proj, proj, w_chunk, sb_full, ln_g.reshape(1, wdt), ln_b.reshape(1, wdt))


def _gdn_prompt_kernel(q_ref, k_ref, v_ref, z_ref, cw_ref, gct_ref, bt_ref, gcr_ref, ng_ref,
                       y_ref, s_ref, xp_scr, s_scr, *, chunk, n_chunks):
    c = chunk
    hw = B_HEADS * B_DK
    step = pl.program_id(1)

    @pl.when(step == 0)
    def _():
        s_scr[...] = jnp.zeros_like(s_scr)
        xp_scr[:, 0:SUBLANE, :] = jnp.zeros((3, SUBLANE, hw), F32)

    act = []
    for idx, r in enumerate((q_ref, k_ref, v_ref)):
        xp_scr[idx, SUBLANE:SUBLANE + c, :] = r[...]
        a = None
        for i in range(CONV_W):
            off = SUBLANE - (CONV_W - 1) + i
            term = xp_scr[idx, off:off + c, :] * cw_ref[i:i + 1, idx * hw:(idx + 1) * hw]
            a = term if a is None else a + term
        xp_scr[idx, 0:SUBLANE, :] = xp_scr[idx, c:c + SUBLANE, :]
        act.append(_silu(a))
    qa, ka, va = act

    gct = gct_ref[...]
    bt = bt_ref[...]
    gcr_all = gcr_ref[...]
    row = lax.broadcasted_iota(jnp.int32, (c, c), 0)
    col = lax.broadcasted_iota(jnp.int32, (c, c), 1)
    causal = row >= col
    strict = row > col
    levels = int(math.log2(c))
    ng = ng_ref[...]

    for h in range(B_HEADS):
        sl = slice(h * B_DK, (h + 1) * B_DK)
        q = qa[:, sl]
        q = q * lax.rsqrt(jnp.sum(q * q, axis=-1, keepdims=True) + EPS) * (B_DK ** -0.5)
        k = ka[:, sl]
        k = k * lax.rsqrt(jnp.sum(k * k, axis=-1, keepdims=True) + EPS)
        v = va[:, sl]
        gcc = gct[:, h:h + 1]
        gcr = gcr_all[h:h + 1, :]
        bc = bt[:, h:h + 1]
        decay = jnp.exp(jnp.where(causal, gcc - gcr, -jnp.inf))
        qkk = _dot_nt(jnp.concatenate([q, k], axis=0), k)
        qk = qkk[:c] * decay
        x = -jnp.where(strict, qkk[c:] * decay * bc, 0.0)
        r_acc = x
        p = x
        for _ in range(levels - 1):
            p = _dot_hi(p, p)
            r_acc = r_acc + p + _dot_hi(r_acc, p)
        egc = jnp.exp(gcc)
        rhs = jnp.concatenate([v * bc, k * (bc * egc)], axis=1)
        uw = rhs + _dot_hi(r_acc, rhs)
        u = uw[:, :B_DK]
        w = uw[:, B_DK:]
        s_old = s_scr[h]
        wq = _dot(jnp.concatenate([w, q * egc], axis=0), s_old)
        v_new = u - wq[:c]
        o = wq[c:] + _dot(qk, v_new)
        g_last = gcr[:, c - 1:c]
        s_scr[h] = s_old * jnp.exp(g_last) + _dot_tn(k * jnp.exp(g_last - gcc), v_new)
        zz = z_ref[:, sl]
        y_ref[:, sl] = (_rms(o) * ng * _silu(zz)).astype(y_ref.dtype)

    @pl.when(step == n_chunks - 1)
    def _():
        s_ref[...] = s_scr[...]


def _gdn_prompt_call(proj3, col0, conv_w, gc_tok, beta_tok, gc_row, norm_g):
    bsz, seq, _ = proj3.shape
    c = math.gcd(seq, GDN_CHUNK)
    n = seq // c
    hw = B_HEADS * B_DK
    kern = functools.partial(_gdn_prompt_kernel, chunk=c, n_chunks=n)
    tok = lambda j: pl.BlockSpec((None, c, hw), lambda b, s: (b, s, col0 + j))
    return pl.pallas_call(
        kern,
        grid=(bsz, n),
        in_specs=[tok(0), tok(1), tok(2), tok(3),
                  pl.BlockSpec((CONV_W, 3 * hw), lambda b, s: (0, 0)),
                  pl.BlockSpec((None, c, B_HEADS), lambda b, s: (b, s, 0)),
                  pl.BlockSpec((None, c, B_HEADS), lambda b, s: (b, s, 0)),
                  pl.BlockSpec((None, None, B_HEADS, c), lambda b, s: (b, s, 0, 0)),
                  pl.BlockSpec((1, B_DK), lambda b, s: (0, 0))],
        out_specs=[pl.BlockSpec((None, c, hw), lambda b, s: (b, s, 0)),
                   pl.BlockSpec((None, B_HEADS, B_DK, B_DK), lambda b, s: (b, 0, 0, 0))],
        out_shape=[jax.ShapeDtypeStruct((bsz, seq, hw), BF16),
                   jax.ShapeDtypeStruct((bsz, B_HEADS, B_DK, B_DK), F32)],
        scratch_shapes=[pltpu.VMEM((3, c + SUBLANE, hw), F32),
                        pltpu.VMEM((B_HEADS, B_DK, B_DK), F32)],
        compiler_params=_cparams(("parallel", "arbitrary")),
        name="gdn_prompt",
    )(proj3, proj3, proj3, proj3, conv_w, gc_tok, beta_tok, gc_row, norm_g.reshape(1, B_DK))


def _gdn_sample_kernel(x_ref, p_ref, z_ref, g_ref, b_ref, s0_ref, cw_ref, ng_ref,
                       y_ref, s_ref, act_scr, o_scr, *, steps, bb):
    hw = B_HEADS * B_DK
    xp = [p_ref[i] for i in range(CONV_W - 1)] + [x_ref[t] for t in range(steps)]
    for t in range(steps):
        a = None
        for i in range(CONV_W):
            term = xp[t + i] * cw_ref[i:i + 1, :]
            a = term if a is None else a + term
        act_scr[t] = _silu(a)

    def head(h, carry):
        off = pl.multiple_of(h * B_DK, B_DK)
        lane_h = lax.broadcasted_iota(jnp.int32, (bb, B_HEADS), 1) == h
        q_t, k_t, v_r, a_c, b_c = [], [], [], [], []
        for t in range(steps):
            q = act_scr[t, :, pl.ds(off, B_DK)]
            q = q * lax.rsqrt(jnp.sum(q * q, axis=-1, keepdims=True) + EPS) * (B_DK ** -0.5)
            k = act_scr[t, :, pl.ds(hw + off, B_DK)]
            k = k * lax.rsqrt(jnp.sum(k * k, axis=-1, keepdims=True) + EPS)
            q_t.append(q.T)
            k_t.append(k.T)
            v_r.append(act_scr[t, :, pl.ds(2 * hw + off, B_DK)])
            a_c.append(jnp.exp(jnp.sum(jnp.where(lane_h, g_ref[t], 0.0), axis=-1, keepdims=True)))
            b_c.append(jnp.sum(jnp.where(lane_h, b_ref[t], 0.0), axis=-1, keepdims=True))
        for b in range(bb):
            s = s0_ref[b, h]
            for t in range(steps):
                a = a_c[t][b:b + 1, :]
                be = b_c[t][b:b + 1, :]
                kc = k_t[t][:, b:b + 1]
                r = jnp.sum(kc * s, axis=0, keepdims=True)
                d = be * (v_r[t][b:b + 1, :] - a * r)
                s = a * s + kc * d
                o_scr[t, b:b + 1, pl.ds(off, B_DK)] = jnp.sum(q_t[t][:, b:b + 1] * s, axis=0, keepdims=True)
            s_ref[b, h] = s
        return carry

    lax.fori_loop(0, B_HEADS, head, 0)

    ng = ng_ref[...]
    for t in range(steps):
        for h in range(B_HEADS):
            sl = slice(h * B_DK, (h + 1) * B_DK)
            y_ref[t, :, sl] = _rms(o_scr[t, :, sl]) * ng * _silu(z_ref[t, :, sl])


def _gdn_sample_call(x_t, prev_t, z_t, g_t, beta_t, s0, conv_w, norm_g):
    steps, bsz, _ = x_t.shape
    hw = B_HEADS * B_DK
    bb = SUBLANE
    kern = functools.partial(_gdn_sample_kernel, steps=steps, bb=bb)
    return pl.pallas_call(
        kern,
        grid=(bsz // bb,),
        in_specs=[pl.BlockSpec((steps, bb, 3 * hw), lambda i: (0, i, 0)),
                  pl.BlockSpec((CONV_W - 1, bb, 3 * hw), lambda i: (0, i, 0)),
                  pl.BlockSpec((steps, bb, hw), lambda i: (0, i, 0)),
                  pl.BlockSpec((steps, bb, B_HEADS), lambda i: (0, i, 0)),
                  pl.BlockSpec((steps, bb, B_HEADS), lambda i: (0, i, 0)),
                  pl.BlockSpec((bb, B_HEADS, B_DK, B_DK), lambda i: (i, 0, 0, 0)),
                  pl.BlockSpec((CONV_W, 3 * hw), lambda i: (0, 0)),
                  pl.BlockSpec((1, B_DK), lambda i: (0, 0))],
        out_specs=[pl.BlockSpec((steps, bb, hw), lambda i: (0, i, 0)),
                   pl.BlockSpec((bb, B_HEADS, B_DK, B_DK), lambda i: (i, 0, 0, 0))],
        out_shape=[jax.ShapeDtypeStruct((steps, bsz, hw), F32),
                   jax.ShapeDtypeStruct((bsz, B_HEADS, B_DK, B_DK), F32)],
        scratch_shapes=[pltpu.VMEM((steps, bb, 3 * hw), F32),
                        pltpu.VMEM((steps, bb, hw), F32)],
        compiler_params=_cparams(("parallel",)),
        name="gdn_sample",
    )(x_t, prev_t, z_t, g_t, beta_t, s0, conv_w, norm_g.reshape(1, B_DK))


def _attn_prompt_kernel(lam_ref, q_ref, k_ref, v_ref, bias_ref, sg_ref, y_ref, m_scr, l_scr, acc_scr,
                        *, tq, tk, out_scale):
    qi = pl.program_id(2)
    dv = 2 * C_DQ
    q = q_ref[...] * (C_DQ ** -0.5)
    lane = lax.broadcasted_iota(jnp.int32, (tq, dv), 1)
    q2 = jnp.concatenate([jnp.where(lane < C_DQ, q, 0.0), jnp.where(lane >= C_DQ, q, 0.0)], axis=0).astype(BF16)
    m_scr[...] = jnp.full_like(m_scr, -jnp.inf)
    l_scr[...] = jnp.zeros_like(l_scr)
    acc_scr[...] = jnp.zeros_like(acc_scr)

    def scores(j):
        start = pl.multiple_of(j * tk, tk)
        bias = bias_ref[jnp.minimum(qi - j, 2)]
        return _dot_nt(q2, k_ref[pl.ds(start, tk), :]) + jnp.concatenate([bias, bias], axis=0)

    def stats(j, carry):
        s = scores(j)
        m_old = m_scr[...]
        m_new = jnp.maximum(m_old, jnp.max(s, axis=-1, keepdims=True))
        l_scr[...] = jnp.exp(m_old - m_new) * l_scr[...] + jnp.sum(jnp.exp(s - m_new), axis=-1, keepdims=True)
        m_scr[...] = m_new
        return carry

    lax.fori_loop(0, qi + 1, stats, 0)
    m = m_scr[...]
    l = l_scr[...]
    lam = lam_ref[0]

    def values(j, carry):
        p = jnp.exp(scores(j) - m) / l
        start = pl.multiple_of(j * tk, tk)
        acc_scr[...] += _dot(p[:tq] - lam * p[tq:], v_ref[pl.ds(start, tk), :])
        return carry

    lax.fori_loop(0, qi + 1, values, 0)
    y_ref[...] = (_rms(acc_scr[...]) * sg_ref[...] * out_scale).astype(y_ref.dtype)


def _attn_prompt_call(proj3, qcol, kcol, vcol, bias_tab, lam, subln_g, out_scale, tq):
    bsz, seq, _ = proj3.shape
    dv = 2 * C_DQ
    kern = functools.partial(_attn_prompt_kernel, tq=tq, tk=tq, out_scale=out_scale)
    return pl.pallas_call(
        kern,
        grid=(bsz, C_HEADS, seq // tq),
        in_specs=[pl.BlockSpec(memory_space=pltpu.SMEM),
                  pl.BlockSpec((None, tq, dv), lambda b, h, i: (b, i, qcol + h)),
                  pl.BlockSpec((None, seq, dv), lambda b, h, i: (b, 0, kcol + h)),
                  pl.BlockSpec((None, seq, dv), lambda b, h, i: (b, 0, vcol + h)),
                  pl.BlockSpec((None, 3, tq, tq), lambda b, h, i: (h, 0, 0, 0)),
                  pl.BlockSpec((1, dv), lambda b, h, i: (0, 0))],
        out_specs=pl.BlockSpec((None, tq, dv), lambda b, h, i: (b, i, h)),
        out_shape=jax.ShapeDtypeStruct((bsz, seq, C_HEADS * dv), BF16),
        scratch_shapes=[pltpu.VMEM((2 * tq, 1), F32), pltpu.VMEM((2 * tq, 1), F32),
                        pltpu.VMEM((tq, dv), F32)],
        compiler_params=_cparams(("parallel", "parallel", "arbitrary")),
        name="attn_prompt",
    )(lam, proj3, proj3, proj3, bias_tab, subln_g.reshape(1, dv))


def _attn_sample_kernel(pt_ref, lam_ref, q_ref, kn_ref, vn_ref, bnew_ref, bpast_ref, sg_ref, *rest,
                        steps, ppg, n_groups, out_scale):
    k_refs = rest[:ppg]
    v_refs = rest[ppg:2 * ppg]
    y_ref, m_scr, l_scr, acc_scr = rest[2 * ppg:]
    g = pl.program_id(1)
    dv = 2 * C_DQ
    width = C_HEADS * dv
    nrow = C_HEADS * 2 * steps

    rr = lax.broadcasted_iota(jnp.int32, (nrow, width), 0)
    cc = lax.broadcasted_iota(jnp.int32, (nrow, width), 1)
    keep = (cc // dv == rr // (2 * steps)) & ((cc // C_DQ) % 2 == (rr // steps) % 2)
    qbd = jnp.where(keep, q_ref[...] * (C_DQ ** -0.5), 0.0)

    @pl.when(g == 0)
    def _():
        s_cols = [jnp.sum(qbd * kn_ref[j:j + 1, :], axis=-1, keepdims=True) for j in range(steps)]
        s_cols = [s_cols[j] + bnew_ref[:, j:j + 1] for j in range(steps)]
        m = functools.reduce(jnp.maximum, s_cols)
        p_cols = [jnp.exp(sc - m) for sc in s_cols]
        m_scr[...] = m
        l_scr[...] = functools.reduce(lambda a, b: a + b, p_cols)
        acc = p_cols[0] * vn_ref[0:1, :]
        for j in range(1, steps):
            acc = acc + p_cols[j] * vn_ref[j:j + 1, :]
        acc_scr[...] = acc

    qb = qbd.astype(BF16)
    for pidx in range(ppg):
        s = _dot_nt(qb, k_refs[pidx][...]) + bpast_ref[pidx]
        m_old = m_scr[...]
        m_new = jnp.maximum(m_old, jnp.max(s, axis=-1, keepdims=True))
        alpha = jnp.exp(m_old - m_new)
        p = jnp.exp(s - m_new)
        l_scr[...] = alpha * l_scr[...] + jnp.sum(p, axis=-1, keepdims=True)
        acc_scr[...] = alpha * acc_scr[...] + _dot(p, v_refs[pidx][...])
        m_scr[...] = m_new

    @pl.when(g == n_groups - 1)
    def _():
        o = acc_scr[...] / l_scr[...]
        lam = lam_ref[0]
        for h in range(C_HEADS):
            r0 = h * 2 * steps
            sl = slice(h * dv, (h + 1) * dv)
            od = o[r0:r0 + steps, sl] - lam * o[r0 + steps:r0 + 2 * steps, sl]
            y_ref[:, sl] = _rms(od) * sg_ref[...] * out_scale


def _attn_sample_call(page_table, lam, q, k_new, v_new, bias_new, bias_past, subln_g, cache_k4, cache_v4,
                      layer, out_scale, ppg):
    bsz, steps, width = q.shape
    n_pages = page_table.shape[1]
    page = cache_k4.shape[2]
    n_groups = n_pages // ppg
    nrow = C_HEADS * 2 * steps
    dv = 2 * C_DQ
    kern = functools.partial(_attn_sample_kernel, steps=steps, ppg=ppg, n_groups=n_groups, out_scale=out_scale)

    def page_spec(pidx):
        return pl.BlockSpec((None, None, page, width),
                            lambda b, g, pt: (layer, pt[b * n_pages + g * ppg + pidx], 0, 0))

    row_spec = pl.BlockSpec((None, steps, width), lambda b, g, pt: (b, 0, 0))
    q_rows = jnp.tile(q, (1, 2 * C_HEADS, 1))
    grid_spec = pltpu.PrefetchScalarGridSpec(
        num_scalar_prefetch=1,
        grid=(bsz, n_groups),
        in_specs=[pl.BlockSpec(memory_space=pltpu.SMEM),
                  pl.BlockSpec((None, nrow, width), lambda b, g, pt: (b, 0, 0)), row_spec, row_spec,
                  pl.BlockSpec((nrow, steps), lambda b, g, pt: (0, 0)),
                  pl.BlockSpec((ppg, nrow, page), lambda b, g, pt: (g, 0, 0)),
                  pl.BlockSpec((1, dv), lambda b, g, pt: (0, 0))]
                 + [page_spec(p) for p in range(ppg)] + [page_spec(p) for p in range(ppg)],
        out_specs=pl.BlockSpec((None, steps, width), lambda b, g, pt: (b, 0, 0)),
        scratch_shapes=[pltpu.VMEM((nrow, 1), F32), pltpu.VMEM((nrow, 1), F32), pltpu.VMEM((nrow, width), F32)],
    )
    return pl.pallas_call(
        kern,
        grid_spec=grid_spec,
        out_shape=jax.ShapeDtypeStruct((bsz, steps, width), F32),
        compiler_params=_cparams(("parallel", "arbitrary")),
        name="attn_sample",
    )(page_table.reshape(-1), lam, q_rows, k_new, v_new, bias_new, bias_past, subln_g.reshape(1, dv),
      *([cache_k4] * ppg), *([cache_v4] * ppg))


def _merge_kernel(ya_ref, yb_ref, yc_ref, ga_ref, gb_ref, gc_ref, w_ref, o_ref):
    acc = None
    for n, (y_ref, g_ref) in enumerate(((ya_ref, ga_ref), (yb_ref, gb_ref), (yc_ref, gc_ref))):
        term = _sigmoid(g_ref[...]) * jnp.dot(y_ref[...].astype(BF16), w_ref[n].astype(BF16),
                                              preferred_element_type=F32)
        acc = term if acc is None else acc + term
    o_ref[...] = acc.astype(o_ref.dtype)


def _merge_call(ya, yb, yc, proj, gate_col0, w_branch, tm, tn):
    t, wdt = ya.shape
    d = w_branch.shape[2]
    nb = d // tn
    y_spec = pl.BlockSpec((tm, wdt), lambda i, j: (i, 0))
    g_spec = lambda n: pl.BlockSpec((tm, tn), lambda i, j: (i, gate_col0 + n * nb + j))
    return pl.pallas_call(
        _merge_kernel,
        grid=(t // tm, nb),
        in_specs=[y_spec, y_spec, y_spec, g_spec(0), g_spec(1), g_spec(2),
                  pl.BlockSpec((N_BRANCH, wdt, tn), lambda i, j: (0, 0, j))],
        out_specs=pl.BlockSpec((tm, tn), lambda i, j: (i, j)),
        out_shape=jax.ShapeDtypeStruct((t, d), BF16),
        compiler_params=_cparams(("parallel", "arbitrary")),
        name="merge",
    )(ya, yb, yc, proj, proj, proj, w_branch)


def _outproj_kernel(m_ref, x_ref, g1_ref, w_ref, o_ref):
    o_ref[...] = x_ref[...] + g1_ref[...] * jnp.dot(m_ref[...], w_ref[...].astype(BF16),
                                                   preferred_element_type=F32)


def _outproj_call(merged, x, grp, layer, gate_chunk, w, tn):
    t, d = x.shape
    tm = grp.tm
    nb = d // tn
    if grp.per_token:
        g1 = pl.BlockSpec((None, tm, tn), lambda i, j: (layer, i, gate_chunk * nb + j))
    else:
        seq = grp.seq
        g1 = pl.BlockSpec((None, None, 1, tn), lambda i, j: (layer, (i * tm) // seq, 0, gate_chunk * nb + j))
    return pl.pallas_call(
        _outproj_kernel,
        grid=(t // tm, nb),
        in_specs=[pl.BlockSpec((tm, d), lambda i, j: (i, 0)),
                  pl.BlockSpec((tm, tn), lambda i, j: (i, j)),
                  g1,
                  pl.BlockSpec((d, tn), lambda i, j: (0, j))],
        out_specs=pl.BlockSpec((tm, tn), lambda i, j: (i, j)),
        out_shape=jax.ShapeDtypeStruct((t, d), F32),
        compiler_params=_cparams(("parallel", "arbitrary")),
        name="outproj",
    )(merged, x, grp.mod_arr, w)


def _router_kernel(x_ref, sc_ref, sh_ref, g_ref, wr_ref, h_ref, s_ref):
    h = _rms(x_ref[...]) * g_ref[...]
    h = h * (1.0 + sc_ref[...]) + sh_ref[...]
    h_ref[...] = h
    s_ref[...] = _sigmoid(_dot(h, wr_ref[...]))


def _router_call(x, grp, layer, chunks, g, w_router):
    t, d = x.shape
    e = w_router.shape[1]
    tm = grp.tm
    return pl.pallas_call(
        _router_kernel,
        grid=(t // tm,),
        in_specs=[pl.BlockSpec((tm, d), lambda i: (i, 0)),
                  grp.mod_spec(layer, chunks[0], d),
                  grp.mod_spec(layer, chunks[1], d),
                  pl.BlockSpec((1, d), lambda i: (0, 0)),
                  pl.BlockSpec((d, e), lambda i: (0, 0))],
        out_specs=[pl.BlockSpec((tm, d), lambda i: (i, 0)),
                   pl.BlockSpec((tm, e), lambda i: (i, 0))],
        out_shape=[jax.ShapeDtypeStruct((t, d), F32), jax.ShapeDtypeStruct((t, e), F32)],
        compiler_params=_cparams(("parallel",)),
        name="router",
    )(x, grp.mod_arr, grp.mod_arr, g.reshape(1, d), w_router)


def _shared_kernel(h_ref, wg_ref, wu_ref, wd_ref, o_ref):
    hb = h_ref[...].astype(BF16)
    a = _silu(jnp.dot(hb, wg_ref[...].astype(BF16), preferred_element_type=F32))
    a = a * jnp.dot(hb, wu_ref[...].astype(BF16), preferred_element_type=F32)
    o_ref[...] = _dot(a, wd_ref[...])


def _shared_call(h, wg, wu, wd, tm):
    t, d = h.shape
    ds = wg.shape[1]
    return pl.pallas_call(
        _shared_kernel,
        grid=(t // tm,),
        in_specs=[pl.BlockSpec((tm, d), lambda i: (i, 0)),
                  pl.BlockSpec((d, ds), lambda i: (0, 0)),
                  pl.BlockSpec((d, ds), lambda i: (0, 0)),
                  pl.BlockSpec((ds, d), lambda i: (0, 0))],
        out_specs=pl.BlockSpec((tm, d), lambda i: (i, 0)),
        out_shape=jax.ShapeDtypeStruct((t, d), F32),
        compiler_params=_cparams(("parallel",)),
        name="shared_expert",
    )(h, wg, wu, wd)


def _row_gather(idx_ref, src_hbm, dst_ref, sem, n_rows):
    def issue(r, carry):
        pltpu.make_async_copy(src_hbm.at[pl.ds(idx_ref[r], 1)], dst_ref.at[pl.ds(r, 1)], sem).start()
        return carry

    lax.fori_loop(0, n_rows, issue, 0)

    def drain(r, carry):
        pltpu.make_async_copy(src_hbm.at[pl.ds(0, 1)], dst_ref.at[pl.ds(r, 1)], sem).wait()
        return carry

    lax.fori_loop(0, n_rows, drain, 0)


def _experts_kernel(be_ref, nact_ref, tok_ref, wt_ref, h_hbm, wg_ref, wu_ref, wd_ref, y_ref,
                    x_scr, wg_scr, wu_scr, wd_scr, sem, *, blk):
    i = pl.program_id(0)
    active = i < nact_ref[0]

    @pl.when(active)
    def _():
        prev = be_ref[jnp.maximum(i - 1, 0)]

        @pl.when((i == 0) | (prev != be_ref[i]))
        def _():
            wg_scr[...] = wg_ref[...].astype(BF16)
            wu_scr[...] = wu_ref[...].astype(BF16)
            wd_scr[...] = wd_ref[...].astype(BF16)

        _row_gather(tok_ref, h_hbm, x_scr, sem, blk)
        xb = x_scr[...].astype(BF16)
        a = _silu(jnp.dot(xb, wg_scr[...], preferred_element_type=F32))
        a = a * jnp.dot(xb, wu_scr[...], preferred_element_type=F32)
        y_ref[...] = jnp.dot(a.astype(BF16), wd_scr[...], preferred_element_type=F32) * wt_ref[...]

    @pl.when(jnp.logical_not(active))
    def _():
        y_ref[...] = jnp.zeros_like(y_ref)


def _experts_call(blk_e, nact, tok_buf, w_buf, h, w_gate, w_up, w_down, layer, blk):
    n_rows = tok_buf.shape[0]
    n_blocks = n_rows // blk
    d = h.shape[1]
    de = w_gate.shape[3]
    kern = functools.partial(_experts_kernel, blk=blk)
    grid_spec = pltpu.PrefetchScalarGridSpec(
        num_scalar_prefetch=2,
        grid=(n_blocks,),
        in_specs=[pl.BlockSpec((None, None, blk), lambda i, be, na: (i, 0, 0), memory_space=pltpu.SMEM),
                  pl.BlockSpec((blk, 1), lambda i, be, na: (i, 0)),
                  pl.BlockSpec(memory_space=pl.ANY),
                  pl.BlockSpec((None, None, d, de), lambda i, be, na: (layer, be[i], 0, 0)),
                  pl.BlockSpec((None, None, d, de), lambda i, be, na: (layer, be[i], 0, 0)),
                  pl.BlockSpec((None, None, de, d), lambda i, be, na: (layer, be[i], 0, 0))],
        out_specs=pl.BlockSpec((blk, d), lambda i, be, na: (i, 0)),
        scratch_shapes=[pltpu.VMEM((blk, d), F32), pltpu.VMEM((d, de), BF16), pltpu.VMEM((d, de), BF16),
                        pltpu.VMEM((de, d), BF16), pltpu.SemaphoreType.DMA(())],
    )
    return pl.pallas_call(
        kern,
        grid_spec=grid_spec,
        out_shape=jax.ShapeDtypeStruct((n_rows, d), F32),
        compiler_params=_cparams(("arbitrary",)),
        name="experts",
    )(blk_e, nact, tok_buf.reshape(n_blocks, 1, blk), w_buf.reshape(n_rows, 1), h, w_gate, w_up, w_down)


def _combine_kernel(pos_ref, x_ref, sh_ref, g2_ref, fg_ref, y_hbm, o_ref, buf, sem, *, tc, final):
    _row_gather(pos_ref, y_hbm, buf, sem, tc * TOP_K)
    acc = buf[pl.ds(0, tc), :]
    for k in range(1, TOP_K):
        acc = acc + buf[pl.ds(k * tc, tc), :]
    out = x_ref[...] + g2_ref[...] * (acc + sh_ref[...])
    o_ref[...] = _rms(out) * fg_ref[...] if final else out


def _combine_call(pos_km, x, shared, grp, layer, gate_chunk, final_g, y_sorted, tc, final):
    t, d = x.shape
    kern = functools.partial(_combine_kernel, tc=tc, final=final)
    tm, seq = grp.tm, grp.seq
    if grp.per_token:
        g2 = pl.BlockSpec((None, tc, d), lambda i: (layer, i, gate_chunk))
    else:
        g2 = pl.BlockSpec((None, None, 1, d), lambda i: (layer, (i * tc) // seq, 0, gate_chunk))
    tile = pl.BlockSpec((tc, d), lambda i: (i, 0))
    return pl.pallas_call(
        kern,
        grid=(t // tc,),
        in_specs=[pl.BlockSpec((None, None, TOP_K * tc), lambda i: (i, 0, 0), memory_space=pltpu.SMEM),
                  tile, tile, g2,
                  pl.BlockSpec((1, d), lambda i: (0, 0)),
                  pl.BlockSpec(memory_space=pl.ANY)],
        out_specs=tile,
        out_shape=jax.ShapeDtypeStruct((t, d), F32),
        scratch_shapes=[pltpu.VMEM((TOP_K * tc, d), F32), pltpu.SemaphoreType.DMA(())],
        compiler_params=_cparams(("arbitrary",)),
        name="combine",
    )(pos_km, x, shared, grp.mod_arr, final_g.reshape(1, d), y_sorted)


def _route(scores, router_bias, n_experts, blk, tc):
    t = scores.shape[0]
    sel = scores + router_bias.astype(F32)
    grp = sel.reshape(t, N_GROUPS, n_experts // N_GROUPS)
    grp_score = jnp.sum(lax.top_k(grp, 2)[0], axis=-1)
    _, top_g = lax.top_k(grp_score, TOPK_GROUPS)
    gmask = jnp.any(top_g[..., None] == jnp.arange(N_GROUPS), axis=1)
    emask = jnp.repeat(gmask, n_experts // N_GROUPS, axis=1)
    _, top_e = lax.top_k(jnp.where(emask, sel, -jnp.inf), TOP_K)
    wts = jnp.take_along_axis(scores, top_e, axis=1)
    wts = wts / jnp.sum(wts, axis=-1, keepdims=True) * ROUTED_SCALE

    n_assign = t * TOP_K
    n_blocks = -(-(n_assign + n_experts * (blk - 1)) // blk)
    n_rows = n_blocks * blk
    e_flat = top_e.reshape(-1).astype(jnp.int32)
    order = jnp.argsort(e_flat)
    e_sorted = e_flat[order]
    counts = jnp.bincount(e_flat, length=n_experts).astype(jnp.int32)
    padded = (counts + blk - 1) // blk * blk
    start = jnp.cumsum(counts) - counts
    pend = jnp.cumsum(padded)
    pstart = pend - padded
    dest = pstart[e_sorted] + jnp.arange(n_assign, dtype=jnp.int32) - start[e_sorted]
    tok_buf = jnp.zeros((n_rows,), jnp.int32).at[dest].set((order // TOP_K).astype(jnp.int32))
    w_buf = jnp.zeros((n_rows,), F32).at[dest].set(wts.reshape(-1)[order])
    blk_start = jnp.arange(n_blocks, dtype=jnp.int32) * blk
    blk_e = jnp.minimum(jnp.searchsorted(pend, blk_start, side='right'), n_experts - 1).astype(jnp.int32)
    nact = (pend[-1] // blk).astype(jnp.int32).reshape(1)
    last_e = blk_e[jnp.maximum(nact[0] - 1, 0)]
    blk_e = jnp.where(blk_start < pend[-1], blk_e, last_e)
    pos = jnp.zeros((n_assign,), jnp.int32).at[order].set(dest).reshape(t, TOP_K)
    pos = jnp.take_along_axis(pos, jnp.argsort(top_e, axis=1), axis=1)
    pos_km = pos.reshape(t // tc, tc, TOP_K).transpose(0, 2, 1).reshape(t // tc, 1, TOP_K * tc)
    return blk_e, nact, tok_buf, w_buf, pos_km


def _bucket_bias(dist, rel_bias):
    n = jnp.maximum(dist, 0)
    exact = N_BUCKETS // 2
    nf = jnp.maximum(n, exact).astype(F32)
    large = exact + (jnp.log(nf / exact) / math.log(MAX_DISTANCE / exact) * (N_BUCKETS - exact)).astype(jnp.int32)
    bucket = jnp.where(n < exact, n, jnp.minimum(large, N_BUCKETS - 1))
    return rel_bias[bucket].astype(F32)


def _prompt_bias_table(rel_bias, tq):
    r = jnp.arange(tq, dtype=jnp.int32)[:, None]
    c = jnp.arange(tq, dtype=jnp.int32)[None, :]
    tabs = []
    for back in range(3):
        dist = back * tq + r - c
        b = jnp.transpose(_bucket_bias(dist, rel_bias), (2, 0, 1))
        tabs.append(jnp.where((dist >= 0)[None], b, -jnp.inf))
    return jnp.stack(tabs, axis=1)


def _sample_bias_tables(rel_bias, steps, past_len, page):
    qpos = past_len + jnp.arange(steps, dtype=jnp.int32)
    kpos = jnp.arange(past_len, dtype=jnp.int32)
    bp = _bucket_bias(qpos[:, None] - kpos[None, :], rel_bias)
    bp = jnp.transpose(bp, (2, 0, 1))
    bp = jnp.broadcast_to(bp[:, None], (C_HEADS, 2, steps, past_len)).reshape(C_HEADS * 2 * steps, past_len)
    bp = bp.reshape(-1, past_len // page, page).transpose(1, 0, 2)
    dn = qpos[:, None] - qpos[None, :]
    bn = jnp.where((dn >= 0)[..., None], _bucket_bias(dn, rel_bias), -jnp.inf)
    bn = jnp.transpose(bn, (2, 0, 1))
    bn = jnp.broadcast_to(bn[:, None], (C_HEADS, 2, steps, steps)).reshape(C_HEADS * 2 * steps, steps)
    return bp, bn


def _trunk(x3, grp, p, past, depth):
    bsz, seq, d = x3.shape
    t = bsz * seq
    is_sample = past is not None
    x = x3.reshape(t, d)
    hw = B_HEADS * B_DK
    wdt = A_GROUPS * LANE
    n_experts = p['w_gate'].shape[1]
    tm = grp.tm
    blk = 256
    tc = min(128, t)
    new_k, new_v, new_s, new_conv, new_va = [], [], [], [], []
    for l in range(depth):
        proj = _inproj_call(x, grp, l, (1, 0), p['norm_mix_g'][l], p['w_in_main'][l], p['tn_in'])
        proj_ab = _inproj_call(x, grp, l, (1, 0), p['norm_mix_g'][l], p['w_in_ab'][l], LANE)
        proj3 = proj.reshape(bsz, seq, -1)

        c_a = min(seq, A_CHUNK)
        causal = jnp.tril(jnp.ones((c_a, c_a), bool))
        w_s = jnp.where(causal[None], p['spatial_w'][l][:, :c_a, :c_a], 0.0)
        sb = p['spatial_b'][l][:, :c_a]
        if c_a < A_CHUNK:
            rep = A_CHUNK // c_a
            w_s = jax.vmap(lambda m: jnp.kron(jnp.eye(rep, dtype=F32), m))(w_s)
            sb = jnp.tile(sb, (1, rep))
        sb_full = jnp.repeat(jnp.transpose(sb), LANE, axis=1)
        y_a, v_norm = _mixa_call(proj, w_s, sb_full, p['ln_v_g'][l], p['ln_v_b'][l])

        a_b = proj_ab[:, :B_HEADS]
        b_b = proj_ab[:, B_HEADS:2 * B_HEADS]
        beta = jax.nn.sigmoid(b_b).reshape(bsz, seq, B_HEADS)
        g_log = (-jnp.exp(p['A_log'][l].astype(F32))
                 * jax.nn.softplus(a_b + p['dt_bias'][l].astype(F32))).reshape(bsz, seq, B_HEADS)
        qkv_raw = proj3[:, :, 2 * wdt:2 * wdt + 3 * hw]
        if is_sample:
            tr = lambda a: jnp.transpose(a, (1, 0, 2))
            y_b_t, s_new = _gdn_sample_call(tr(qkv_raw), tr(past['state_conv'][l]),
                                            tr(proj3[:, :, 2 * wdt + 3 * hw:2 * wdt + 4 * hw]),
                                            tr(g_log), tr(beta), past['state_gdn'][l], p['conv_w'][l],
                                            p['gdn_norm_g'][l])
            y_b = tr(y_b_t).reshape(t, hw).astype(BF16)
            conv_state = jnp.concatenate([past['state_conv'][l], qkv_raw], axis=1)[:, seq:]
        else:
            c_g = math.gcd(seq, GDN_CHUNK)
            gc_row = jnp.cumsum(jnp.transpose(g_log.reshape(bsz, seq // c_g, c_g, B_HEADS), (0, 1, 3, 2)), axis=-1)
            gc_tok = jnp.transpose(gc_row, (0, 1, 3, 2)).reshape(bsz, seq, B_HEADS)
            y_b3, s_new = _gdn_prompt_call(proj3, 2 * wdt // hw, p['conv_w'][l], gc_tok, beta, gc_row,
                                           p['gdn_norm_g'][l])
            y_b = y_b3.reshape(t, hw)
            conv_state = jnp.concatenate([jnp.zeros((bsz, CONV_W - 1, 3 * hw), F32), qkv_raw], axis=1)[:, seq:]

        c0 = 2 * wdt + 4 * hw
        cw = C_HEADS * 2 * C_DQ
        k_c = proj3[:, :, c0 + cw:c0 + 2 * cw]
        v_c = proj3[:, :, c0 + 2 * cw:c0 + 3 * cw]
        lam_init = 0.8 - 0.6 * math.exp(-0.3 * l)
        lam = (jnp.exp(jnp.sum(p['lambda_q1'][l].astype(F32) * p['lambda_k1'][l].astype(F32)))
               - jnp.exp(jnp.sum(p['lambda_q2'][l].astype(F32) * p['lambda_k2'][l].astype(F32))) + lam_init)
        lam = lam.reshape(1).astype(F32)
        if is_sample:
            y_c = _attn_sample_call(past['page_table'], lam, proj3[:, :, c0:c0 + cw], k_c, v_c,
                                    p['bias_new'], p['bias_past'], p['subln_g'][l], past['cache_k4'],
                                    past['cache_v4'], l, 1.0 - lam_init, p['ppg'])
            y_c = y_c.reshape(t, cw).astype(BF16)
        else:
            nb = c0 // (2 * C_DQ)
            y_c = _attn_prompt_call(proj3, nb, nb + C_HEADS, nb + 2 * C_HEADS, p['bias_prompt'], lam,
                                    p['subln_g'][l], 1.0 - lam_init, p['tq']).reshape(t, cw)

        gate0 = c0 + 3 * cw
        tn_m = 512
        merged = _merge_call(y_a, y_b, y_c, proj, gate0 // tn_m, p['w_branch'][l], tm, tn_m)
        x = _outproj_call(merged, x, grp, l, 2, p['w_out'][l], 512)

        h2, scores = _router_call(x, grp, l, (4, 3), p['norm_moe_g'][l], p['w_router'][l])
        blk_e, nact, tok_buf, w_buf, pos_km = _route(scores, p['router_bias'][l], n_experts, blk, tc)
        y_sorted = _experts_call(blk_e, nact, tok_buf, w_buf, h2, p['w_gate'], p['w_up'], p['w_down'], l, blk)
        shared = _shared_call(h2, p['ws_gate'][l], p['ws_up'][l], p['ws_down'][l], tm)
        x = _combine_call(pos_km, x, shared, grp, l, 5, p['final_norm_g'], y_sorted, tc, l == depth - 1)

        new_k.append(k_c.reshape(bsz, seq, C_HEADS, 2 * C_DQ))
        new_v.append(v_c.reshape(bsz, seq, C_HEADS, 2 * C_DQ))
        new_s.append(s_new)
        new_conv.append(conv_state)
        if is_sample:
            new_va.append(v_norm.reshape(bsz, seq, wdt))
    chunk_v = jnp.stack(new_va) if is_sample else None
    return (x.reshape(bsz, seq, d), jnp.stack(new_k), jnp.stack(new_v), jnp.stack(new_s), jnp.stack(new_conv),
            chunk_v)


def kernel(x_prompt, x_sample, cache_k, cache_v, state_gdn, state_conv, page_table, c_prompt, c_sample,
           w_mod, b_mod, norm_mix_g, w_in, spatial_w, spatial_b, ln_v_g, ln_v_b, conv_w, A_log, dt_bias,
           gdn_norm_g, lambda_q1, lambda_k1, lambda_q2, lambda_k2, subln_g, rel_bias, w_branch, w_out,
           norm_moe_g, w_router, router_bias, w_gate, w_up, w_down, ws_gate, ws_up, ws_down, final_norm_g):
    depth = w_mod.shape[0]
    bsz, seq, d = x_prompt.shape
    dbsz, dseq, _ = x_sample.shape
    wdt = A_GROUPS * LANE
    hw = B_HEADS * B_DK
    n_ab = 2 * B_HEADS
    ab0 = 2 * wdt + 4 * hw
    w_in_main = jnp.concatenate([w_in[:, :, :ab0], w_in[:, :, ab0 + n_ab:]], axis=2)
    w_in_ab = jnp.pad(w_in[:, :, ab0:ab0 + n_ab], ((0, 0), (0, 0), (0, LANE - n_ab)))
    n_main = w_in_main.shape[2]
    tn_in = next(c for c in (1536, 1280, 1024, 512, 256, 128) if n_main % c == 0)

    mod = _mod_call(jnp.concatenate([c_prompt, c_sample], axis=0), w_mod, b_mod)
    mod_prompt = mod[:, :bsz].reshape(depth, bsz, 1, mod.shape[2])
    mod_sample = jnp.repeat(mod[:, bsz:], dseq, axis=1)

    past_len = page_table.shape[1] * cache_k.shape[2]
    tq = min(128, seq)
    bias_past, bias_new = _sample_bias_tables(rel_bias, dseq, past_len, cache_k.shape[2])
    p = dict(norm_mix_g=norm_mix_g, w_in_main=w_in_main, w_in_ab=w_in_ab, tn_in=tn_in, spatial_w=spatial_w,
             spatial_b=spatial_b, ln_v_g=ln_v_g, ln_v_b=ln_v_b, conv_w=conv_w, A_log=A_log, dt_bias=dt_bias,
             gdn_norm_g=gdn_norm_g, lambda_q1=lambda_q1, lambda_k1=lambda_k1, lambda_q2=lambda_q2,
             lambda_k2=lambda_k2, subln_g=subln_g, w_branch=w_branch, w_out=w_out, norm_moe_g=norm_moe_g,
             w_router=w_router, router_bias=router_bias, w_gate=w_gate, w_up=w_up, w_down=w_down,
             ws_gate=ws_gate, ws_up=ws_up, ws_down=ws_down, final_norm_g=final_norm_g,
             bias_prompt=_prompt_bias_table(rel_bias, tq), tq=tq, bias_past=bias_past, bias_new=bias_new,
             ppg=math.gcd(page_table.shape[1], 4))
    past = dict(cache_k4=cache_k.reshape(cache_k.shape[0], cache_k.shape[1], cache_k.shape[2], -1),
                cache_v4=cache_v.reshape(cache_v.shape[0], cache_v.shape[1], cache_v.shape[2], -1),
                state_gdn=state_gdn, state_conv=state_conv, page_table=page_table)

    grp_p = _Group(mod_prompt, False, seq, min(512, seq))
    grp_s = _Group(mod_sample, True, dseq, min(512, dbsz * dseq))
    y_p, k_p, v_p, s_p, conv_p, _ = _trunk(x_prompt, grp_p, p, None, depth)
    y_s, k_s, v_s, s_s, conv_s, chunk_v = _trunk(x_sample, grp_s, p, past, depth)
    return (y_p, y_s, k_p, v_p, k_s, v_s, s_p, s_s, conv_p, conv_s, chunk_v)
```

```python
import functools
import math

import numpy as np
import jax
import jax.numpy as jnp
from jax import lax
from jax.experimental import pallas as pl
from jax.experimental.pallas import tpu as pltpu

F32 = jnp.float32
BF16 = jnp.bfloat16
EPS = 1e-6

LANE = 128
SUBLANE = 8
VMEM_LIMIT_BYTES = 56 * 1024 * 1024

A_GROUPS = 8
A_CHUNK = 128
B_HEADS = 8
B_DK = 128
CONV_W = 4
GDN_CHUNK = 64
C_HEADS = 8
C_DQ = 64
N_BUCKETS = 32
MAX_DISTANCE = 128
TOP_K = 8
N_GROUPS = 8
TOPK_GROUPS = 4
ROUTED_SCALE = 2.5
N_BRANCH = 3
GELU_C = float(np.float32(np.sqrt(2.0 / np.pi)))


def _cparams(sem):
    return pltpu.CompilerParams(dimension_semantics=sem, vmem_limit_bytes=VMEM_LIMIT_BYTES)


def _sigmoid(x):
    return 1.0 / (1.0 + jnp.exp(-x))


def _silu(x):
    return x * _sigmoid(x)


def _gelu(x):
    return x * (0.5 * (1.0 + jnp.tanh(GELU_C * (x + 0.044715 * (x * x * x)))))


def _dot(a, b):
    return jnp.dot(a.astype(BF16), b.astype(BF16), preferred_element_type=F32)


def _dot_nt(a, b):
    return lax.dot_general(a.astype(BF16), b.astype(BF16), (((1,), (1,)), ((), ())),
                           preferred_element_type=F32)


def _dot_tn(a, b):
    return lax.dot_general(a.astype(BF16), b.astype(BF16), (((0,), (0,)), ((), ())),
                           preferred_element_type=F32)


def _split2(a):
    hi = a.astype(BF16)
    return hi, (a - hi.astype(F32)).astype(BF16)


def _dot_hi(a, b):
    a0, a1 = _split2(a)
    b0, b1 = _split2(b)
    d = lambda x, y: jnp.dot(x, y, preferred_element_type=F32)
    return (d(a1, b0) + d(a0, b1)) + d(a0, b0)


def _rms(x):
    return x * lax.rsqrt(jnp.mean(x * x, axis=-1, keepdims=True) + EPS)


def _mod_kernel(c_ref, w_ref, b_ref, o_ref):
    o_ref[...] = _dot(_silu(c_ref[...]), w_ref[...]) + b_ref[...]


def _mod_call(c_all, w_mod, b_mod):
    depth, d, n = w_mod.shape
    m = c_all.shape[0]
    tn = 1536 if n % 1536 == 0 else n
    return pl.pallas_call(
        _mod_kernel,
        grid=(depth, n // tn),
        in_specs=[pl.BlockSpec((m, d), lambda l, j: (0, 0)),
                  pl.BlockSpec((None, d, tn), lambda l, j: (l, 0, j)),
                  pl.BlockSpec((None, 1, tn), lambda l, j: (l, 0, j))],
        out_specs=pl.BlockSpec((None, m, tn), lambda l, j: (l, 0, j)),
        out_shape=jax.ShapeDtypeStruct((depth, m, n), F32),
        compiler_params=_cparams(("parallel", "parallel")),
        name="mod",
    )(c_all, w_mod, b_mod.reshape(depth, 1, n))


class _Group:
    def __init__(self, mod_arr, per_token, seq, tm):
        self.mod_arr = mod_arr
        self.per_token = per_token
        self.seq = seq
        self.tm = tm

    def mod_spec(self, layer, chunk, d):
        tm, seq = self.tm, self.seq
        if self.per_token:
            return pl.BlockSpec((None, tm, d), lambda i, *_: (layer, i, chunk))
        return pl.BlockSpec((None, None, 1, d), lambda i, *_: (layer, (i * tm) // seq, 0, chunk))


def _inproj_kernel(x_ref, sc_ref, sh_ref, g_ref, w_ref, o_ref, h_scr):
    @pl.when(pl.program_id(1) == 0)
    def _():
        h = _rms(x_ref[...]) * g_ref[...]
        h_scr[...] = (h * (1.0 + sc_ref[...]) + sh_ref[...]).astype(BF16)

    o_ref[...] = jnp.dot(h_scr[...], w_ref[...].astype(BF16), preferred_element_type=F32)


def _inproj_call(x, grp, layer, chunks, g, w, tn):
    t, d = x.shape
    n = w.shape[1]
    tm = grp.tm
    return pl.pallas_call(
        _inproj_kernel,
        grid=(t // tm, n // tn),
        in_specs=[pl.BlockSpec((tm, d), lambda i, j: (i, 0)),
                  grp.mod_spec(layer, chunks[0], d),
                  grp.mod_spec(layer, chunks[1], d),
                  pl.BlockSpec((1, d), lambda i, j: (0, 0)),
                  pl.BlockSpec((d, tn), lambda i, j: (0, j))],
        out_specs=pl.BlockSpec((tm, tn), lambda i, j: (i, j)),
        out_shape=jax.ShapeDtypeStruct((t, n), F32),
        scratch_shapes=[pltpu.VMEM((tm, d), BF16)],
        compiler_params=_cparams(("parallel", "arbitrary")),
        name="inproj",
    )(x, grp.mod_arr, grp.mod_arr, g.reshape(1, d), w)


def _mixa_kernel(u_ref, v_ref, w_ref, sb_ref, lg_ref, lb_ref, y_ref, vn_ref):
    v = _gelu(v_ref[...])
    xc = v - jnp.mean(v, axis=-1, keepdims=True)
    vn = xc * lax.rsqrt(jnp.mean(xc * xc, axis=-1, keepdims=True) + EPS) * lg_ref[...] + lb_ref[...]
    vn_ref[...] = vn
    vb = vn.astype(BF16)
    gd = LANE
    for g in range(A_GROUPS):
        sl = slice(g * gd, (g + 1) * gd)
        s = jnp.dot(w_ref[g].astype(BF16), vb[:, sl], preferred_element_type=F32) + sb_ref[:, sl]
        y_ref[:, sl] = (_gelu(u_ref[:, sl]) * s).astype(y_ref.dtype)


def _mixa_call(proj, w_chunk, sb_full, ln_g, ln_b):
    t = proj.shape[0]
    wdt = A_GROUPS * LANE
    c = A_CHUNK
    return pl.pallas_call(
        _mixa_kernel,
        grid=(t // c,),
        in_specs=[pl.BlockSpec((c, wdt), lambda i: (i, 0)),
                  pl.BlockSpec((c, wdt), lambda i: (i, 1)),
                  pl.BlockSpec((A_GROUPS, c, c), lambda i: (0, 0, 0)),
                  pl.BlockSpec((c, wdt), lambda i: (0, 0)),
                  pl.BlockSpec((1, wdt), lambda i: (0, 0)),
                  pl.BlockSpec((1, wdt), lambda i: (0, 0))],
        out_specs=[pl.BlockSpec((c, wdt), lambda i: (i, 0)),
                   pl.BlockSpec((c, wdt), lambda i: (i, 0))],
        out_shape=[jax.ShapeDtypeStruct((t, wdt), BF16),
                   jax.ShapeDtypeStruct((t, wdt), F32)],
        compiler_params=_cparams(("parallel",)),
        name="mixer_a",
    )(proj, proj, w_chunk, sb_full, ln_g.reshape(1, wdt), ln_b.reshape(1, wdt))


def _gdn_prompt_kernel(q_ref, k_ref, v_ref, z_ref, cw_ref, gct_ref, bt_ref, gcr_ref, ng_ref,
                       y_ref, s_ref, xp_scr, s_scr, *, chunk, n_chunks):
    c = chunk
    hw = B_HEADS * B_DK
    step = pl.program_id(1)

    @pl.when(step == 0)
    def _():
        s_scr[...] = jnp.zeros_like(s_scr)
        xp_scr[:, 0:SUBLANE, :] = jnp.zeros((3, SUBLANE, hw), F32)

    act = []
    for idx, r in enumerate((q_ref, k_ref, v_ref)):
        xp_scr[idx, SUBLANE:SUBLANE + c, :] = r[...]
        a = None
        for i in range(CONV_W):
            off = SUBLANE - (CONV_W - 1) + i
            term = xp_scr[idx, off:off + c, :] * cw_ref[i:i + 1, idx * hw:(idx + 1) * hw]
            a = term if a is None else a + term
        xp_scr[idx, 0:SUBLANE, :] = xp_scr[idx, c:c + SUBLANE, :]
        act.append(_silu(a))
    qa, ka, va = act

    gct = gct_ref[...]
    bt = bt_ref[...]
    gcr_all = gcr_ref[...]
    row = lax.broadcasted_iota(jnp.int32, (c, c), 0)
    col = lax.broadcasted_iota(jnp.int32, (c, c), 1)
    causal = row >= col
    strict = row > col
    levels = int(math.log2(c))
    ng = ng_ref[...]

    for h in range(B_HEADS):
        sl = slice(h * B_DK, (h + 1) * B_DK)
        q = qa[:, sl]
        q = q * lax.rsqrt(jnp.sum(q * q, axis=-1, keepdims=True) + EPS) * (B_DK ** -0.5)
        k = ka[:, sl]
        k = k * lax.rsqrt(jnp.sum(k * k, axis=-1, keepdims=True) + EPS)
        v = va[:, sl]
        gcc = gct[:, h:h + 1]
        gcr = gcr_all[h:h + 1, :]
        bc = bt[:, h:h + 1]
        decay = jnp.exp(jnp.where(causal, gcc - gcr, -jnp.inf))
        qkk = _dot_nt(jnp.concatenate([q, k], axis=0), k)
        qk = qkk[:c] * decay
        x = -jnp.where(strict, qkk[c:] * decay * bc, 0.0)
        r_acc = x
        p = x
        for _ in range(levels - 1):
            p = _dot_hi(p, p)
            r_acc = r_acc + p + _dot_hi(r_acc, p)
        egc = jnp.exp(gcc)
        rhs = jnp.concatenate([v * bc, k * (bc * egc)], axis=1)
        uw = rhs + _dot_hi(r_acc, rhs)
        u = uw[:, :B_DK]
        w = uw[:, B_DK:]
        s_old = s_scr[h]
        wq = _dot(jnp.concatenate([w, q * egc], axis=0), s_old)
        v_new = u - wq[:c]
        o = wq[c:] + _dot(qk, v_new)
        g_last = gcr[:, c - 1:c]
        s_scr[h] = s_old * jnp.exp(g_last) + _dot_tn(k * jnp.exp(g_last - gcc), v_new)
        zz = z_ref[:, sl]
        y_ref[:, sl] = (_rms(o) * ng * _silu(zz)).astype(y_ref.dtype)

    @pl.when(step == n_chunks - 1)
    def _():
        s_ref[...] = s_scr[...]


def _gdn_prompt_call(proj3, col0, conv_w, gc_tok, beta_tok, gc_row, norm_g):
    bsz, seq, _ = proj3.shape
    c = math.gcd(seq, GDN_CHUNK)
    n = seq // c
    hw = B_HEADS * B_DK
    kern = functools.partial(_gdn_prompt_kernel, chunk=c, n_chunks=n)
    tok = lambda j: pl.BlockSpec((None, c, hw), lambda b, s: (b, s, col0 + j))
    return pl.pallas_call(
        kern,
        grid=(bsz, n),
        in_specs=[tok(0), tok(1), tok(2), tok(3),
                  pl.BlockSpec((CONV_W, 3 * hw), lambda b, s: (0, 0)),
                  pl.BlockSpec((None, c, B_HEADS), lambda b, s: (b, s, 0)),
                  pl.BlockSpec((None, c, B_HEADS), lambda b, s: (b, s, 0)),
                  pl.BlockSpec((None, None, B_HEADS, c), lambda b, s: (b, s, 0, 0)),
                  pl.BlockSpec((1, B_DK), lambda b, s: (0, 0))],
        out_specs=[pl.BlockSpec((None, c, hw), lambda b, s: (b, s, 0)),
                   pl.BlockSpec((None, B_HEADS, B_DK, B_DK), lambda b, s: (b, 0, 0, 0))],
        out_shape=[jax.ShapeDtypeStruct((bsz, seq, hw), BF16),
                   jax.ShapeDtypeStruct((bsz, B_HEADS, B_DK, B_DK), F32)],
        scratch_shapes=[pltpu.VMEM((3, c + SUBLANE, hw), F32),
                        pltpu.VMEM((B_HEADS, B_DK, B_DK), F32)],
        compiler_params=_cparams(("parallel", "arbitrary")),
        name="gdn_prompt",
    )(proj3, proj3, proj3, proj3, conv_w, gc_tok, beta_tok, gc_row, norm_g.reshape(1, B_DK))


def _gdn_sample_kernel(x_ref, p_ref, z_ref, g_ref, b_ref, s0_ref, cw_ref, ng_ref,
                       y_ref, s_ref, act_scr, o_scr, *, steps, bb):
    hw = B_HEADS * B_DK
    xp = [p_ref[i] for i in range(CONV_W - 1)] + [x_ref[t] for t in range(steps)]
    for t in range(steps):
        a = None
        for i in range(CONV_W):
            term = xp[t + i] * cw_ref[i:i + 1, :]
            a = term if a is None else a + term
        act_scr[t] = _silu(a)

    def head(h, carry):
        off = pl.multiple_of(h * B_DK, B_DK)
        lane_h = lax.broadcasted_iota(jnp.int32, (bb, B_HEADS), 1) == h
        q_t, k_t, v_r, a_c, b_c = [], [], [], [], []
        for t in range(steps):
            q = act_scr[t, :, pl.ds(off, B_DK)]
            q = q * lax.rsqrt(jnp.sum(q * q, axis=-1, keepdims=True) + EPS) * (B_DK ** -0.5)
            k = act_scr[t, :, pl.ds(hw + off, B_DK)]
            k = k * lax.rsqrt(jnp.sum(k * k, axis=-1, keepdims=True) + EPS)
            q_t.append(q.T)
            k_t.append(k.T)
            v_r.append(act_scr[t, :, pl.ds(2 * hw + off, B_DK)])
            a_c.append(jnp.exp(jnp.sum(jnp.where(lane_h, g_ref[t], 0.0), axis=-1, keepdims=True)))
            b_c.append(jnp.sum(jnp.where(lane_h, b_ref[t], 0.0), axis=-1, keepdims=True))
        for b in range(bb):
            s = s0_ref[b, h]
            for t in range(steps):
                a = a_c[t][b:b + 1, :]
                be = b_c[t][b:b + 1, :]
                kc = k_t[t][:, b:b + 1]
                r = jnp.sum(kc * s, axis=0, keepdims=True)
                d = be * (v_r[t][b:b + 1, :] - a * r)
                s = a * s + kc * d
                o_scr[t, b:b + 1, pl.ds(off, B_DK)] = jnp.sum(q_t[t][:, b:b + 1] * s, axis=0, keepdims=True)
            s_ref[b, h] = s
        return carry

    lax.fori_loop(0, B_HEADS, head, 0)

    ng = ng_ref[...]
    for t in range(steps):
        for h in range(B_HEADS):
            sl = slice(h * B_DK, (h + 1) * B_DK)
            y_ref[t, :, sl] = _rms(o_scr[t, :, sl]) * ng * _silu(z_ref[t, :, sl])


def _gdn_sample_call(x_t, prev_t, z_t, g_t, beta_t, s0, conv_w, norm_g):
    steps, bsz, _ = x_t.shape
    hw = B_HEADS * B_DK
    bb = SUBLANE
    kern = functools.partial(_gdn_sample_kernel, steps=steps, bb=bb)
    return pl.pallas_call(
        kern,
        grid=(bsz // bb,),
        in_specs=[pl.BlockSpec((steps, bb, 3 * hw), lambda i: (0, i, 0)),
                  pl.BlockSpec((CONV_W - 1, bb, 3 * hw), lambda i: (0, i, 0)),
                  pl.BlockSpec((steps, bb, hw), lambda i: (0, i, 0)),
                  pl.BlockSpec((steps, bb, B_HEADS), lambda i: (0, i, 0)),
                  pl.BlockSpec((steps, bb, B_HEADS), lambda i: (0, i, 0)),
                  pl.BlockSpec((bb, B_HEADS, B_DK, B_DK), lambda i: (i, 0, 0, 0)),
                  pl.BlockSpec((CONV_W, 3 * hw), lambda i: (0, 0)),
                  pl.BlockSpec((1, B_DK), lambda i: (0, 0))],
        out_specs=[pl.BlockSpec((steps, bb, hw), lambda i: (0, i, 0)),
                   pl.BlockSpec((bb, B_HEADS, B_DK, B_DK), lambda i: (i, 0, 0, 0))],
        out_shape=[jax.ShapeDtypeStruct((steps, bsz, hw), F32),
                   jax.ShapeDtypeStruct((bsz, B_HEADS, B_DK, B_DK), F32)],
        scratch_shapes=[pltpu.VMEM((steps, bb, 3 * hw), F32),
                        pltpu.VMEM((steps, bb, hw), F32)],
        compiler_params=_cparams(("parallel",)),
        name="gdn_sample",
    )(x_t, prev_t, z_t, g_t, beta_t, s0, conv_w, norm_g.reshape(1, B_DK))


def _attn_prompt_kernel(lam_ref, q_ref, k_ref, v_ref, bias_ref, sg_ref, y_ref, m_scr, l_scr, acc_scr,
                        *, tq, tk, out_scale):
    qi = pl.program_id(2)
    dv = 2 * C_DQ
    q = q_ref[...] * (C_DQ ** -0.5)
    lane = lax.broadcasted_iota(jnp.int32, (tq, dv), 1)
    q2 = jnp.concatenate([jnp.where(lane < C_DQ, q, 0.0), jnp.where(lane >= C_DQ, q, 0.0)], axis=0).astype(BF16)
    ratio = tq // tk
    n_blk = (qi + 1) * ratio

    def scores(j):
        start = pl.multiple_of(j * tk, tk)
        bias = bias_ref[jnp.minimum(qi * ratio - j + (ratio - 1), ratio + 1)]
        return _dot_nt(q2, k_ref[pl.ds(start, tk), :]) + jnp.concatenate([bias, bias], axis=0)

    m_scr[...] = jnp.full_like(m_scr, -jnp.inf)

    def sweep_max(j, carry):
        m_scr[...] = jnp.maximum(m_scr[...], scores(j))
        return carry

    lax.fori_loop(0, n_blk, sweep_max, 0)
    m_scr[...] = jnp.broadcast_to(jnp.max(m_scr[...], axis=-1, keepdims=True), m_scr.shape)
    l_scr[...] = jnp.zeros_like(l_scr)

    def sweep_sum(j, carry):
        l_scr[...] += jnp.exp(scores(j) - m_scr[...])
        return carry

    lax.fori_loop(0, n_blk, sweep_sum, 0)
    l_scr[...] = jnp.broadcast_to(1.0 / jnp.sum(l_scr[...], axis=-1, keepdims=True), l_scr.shape)
    acc_scr[...] = jnp.zeros_like(acc_scr)
    lam = lam_ref[0]

    def sweep_values(j, carry):
        p = jnp.exp(scores(j) - m_scr[...]) * l_scr[...]
        start = pl.multiple_of(j * tk, tk)
        acc_scr[...] += _dot(p[:tq] - lam * p[tq:], v_ref[pl.ds(start, tk), :])
        return carry

    lax.fori_loop(0, n_blk, sweep_values, 0)
    y_ref[...] = (_rms(acc_scr[...]) * sg_ref[...] * out_scale).astype(y_ref.dtype)


def _attn_prompt_call(proj3, qcol, kcol, vcol, bias_tab, lam, subln_g, out_scale, tq):
    bsz, seq, _ = proj3.shape
    dv = 2 * C_DQ
    tk = bias_tab.shape[3]
    kern = functools.partial(_attn_prompt_kernel, tq=tq, tk=tk, out_scale=out_scale)
    return pl.pallas_call(
        kern,
        grid=(bsz, C_HEADS, seq // tq),
        in_specs=[pl.BlockSpec(memory_space=pltpu.SMEM),
                  pl.BlockSpec((None, tq, dv), lambda b, h, i: (b, i, qcol + h)),
                  pl.BlockSpec((None, seq, dv), lambda b, h, i: (b, 0, kcol + h)),
                  pl.BlockSpec((None, seq, dv), lambda b, h, i: (b, 0, vcol + h)),
                  pl.BlockSpec((None,) + bias_tab.shape[1:], lambda b, h, i: (h, 0, 0, 0)),
                  pl.BlockSpec((1, dv), lambda b, h, i: (0, 0))],
        out_specs=pl.BlockSpec((None, tq, dv), lambda b, h, i: (b, i, h)),
        out_shape=jax.ShapeDtypeStruct((bsz, seq, C_HEADS * dv), BF16),
        scratch_shapes=[pltpu.VMEM((2 * tq, tk), F32), pltpu.VMEM((2 * tq, tk), F32),
                        pltpu.VMEM((tq, dv), F32)],
        compiler_params=_cparams(("parallel", "parallel", "arbitrary")),
        name="attn_prompt",
    )(lam, proj3, proj3, proj3, bias_tab, subln_g.reshape(1, dv))


def _attn_sample_kernel(pt_ref, lam_ref, q_ref, kn_ref, vn_ref, bnew_ref, bias_ref, sg_ref, *rest,
                        steps, ppg, n_groups, out_scale):
    k_refs = rest[:ppg]
    v_refs = rest[ppg:2 * ppg]
    y_ref, m_scr, l_scr, acc_scr = rest[2 * ppg:]
    g = pl.program_id(1)
    dv = 2 * C_DQ
    nrow = C_HEADS * 2 * steps
    rr = lax.broadcasted_iota(jnp.int32, (nrow, dv), 0)
    cc = lax.broadcasted_iota(jnp.int32, (nrow, dv), 1)
    q = jnp.where(cc // C_DQ == (rr // steps) % 2, q_ref[...] * (C_DQ ** -0.5), 0.0)

    @pl.when(g == 0)
    def _():
        s_cols = [jnp.sum(q * kn_ref[j], axis=-1, keepdims=True) + bnew_ref[:, j:j + 1] for j in range(steps)]
        m = functools.reduce(jnp.maximum, s_cols)
        p_cols = [jnp.exp(sc - m) for sc in s_cols]
        m_scr[...] = m
        l_scr[...] = functools.reduce(lambda a, b: a + b, p_cols)
        acc = p_cols[0] * vn_ref[0]
        for j in range(1, steps):
            acc = acc + p_cols[j] * vn_ref[j]
        acc_scr[...] = acc

    qb = q.astype(BF16)
    for pidx in range(ppg):
        is_last = jnp.logical_and(g == n_groups - 1, pidx == ppg - 1).astype(jnp.int32)
        s = _dot_nt(qb, k_refs[pidx][...]) + bias_ref[is_last]
        m_old = m_scr[...]
        m_new = jnp.maximum(m_old, jnp.max(s, axis=-1, keepdims=True))
        alpha = jnp.exp(m_old - m_new)
        p = jnp.exp(s - m_new)
        l_scr[...] = alpha * l_scr[...] + jnp.sum(p, axis=-1, keepdims=True)
        acc_scr[...] = alpha * acc_scr[...] + _dot(p, v_refs[pidx][...])
        m_scr[...] = m_new

    @pl.when(g == n_groups - 1)
    def _():
        o = acc_scr[...] / l_scr[...]
        lam = lam_ref[0]
        for h in range(C_HEADS):
            r0 = h * 2 * steps
            od = o[r0:r0 + steps] - lam * o[r0 + steps:r0 + 2 * steps]
            y_ref[:, h * dv:(h + 1) * dv] = _rms(od) * sg_ref[...] * out_scale


def _attn_sample_call(page_table, lam, q, k_new, v_new, bias_new, bias_past, subln_g, cache_k, cache_v,
                      layer, out_scale, ppg):
    bsz, steps, width = q.shape
    n_pages = page_table.shape[1]
    depth, n_pool, page, n_heads, dv = cache_k.shape
    n_groups = n_pages // ppg
    nrow = n_heads * 2 * steps
    kern = functools.partial(_attn_sample_kernel, steps=steps, ppg=ppg, n_groups=n_groups, out_scale=out_scale)

    def rows(a):
        a = a.reshape(bsz, steps, n_heads, 1, dv)
        return jnp.broadcast_to(a, (bsz, steps, n_heads, 2 * steps, dv)).reshape(bsz, steps, nrow, dv)

    q_rows = jnp.transpose(q.reshape(bsz, steps, n_heads, dv), (0, 2, 1, 3))
    q_rows = jnp.broadcast_to(q_rows[:, :, None], (bsz, n_heads, 2, steps, dv)).reshape(bsz, nrow, dv)
    pages_k = cache_k.reshape(depth, n_pool, page * n_heads, dv)
    pages_v = cache_v.reshape(depth, n_pool, page * n_heads, dv)

    def page_spec(pidx):
        return pl.BlockSpec((None, None, page * n_heads, dv),
                            lambda b, g, pt: (layer, pt[b * n_pages + g * ppg + pidx], 0, 0))

    new_spec = pl.BlockSpec((None, steps, nrow, dv), lambda b, g, pt: (b, 0, 0, 0))
    grid_spec = pltpu.PrefetchScalarGridSpec(
        num_scalar_prefetch=1,
        grid=(bsz, n_groups),
        in_specs=[pl.BlockSpec(memory_space=pltpu.SMEM),
                  pl.BlockSpec((None, nrow, dv), lambda b, g, pt: (b, 0, 0)), new_spec, new_spec,
                  pl.BlockSpec((nrow, steps), lambda b, g, pt: (0, 0)),
                  pl.BlockSpec((2, nrow, page * n_heads), lambda b, g, pt: (0, 0, 0)),
                  pl.BlockSpec((1, dv), lambda b, g, pt: (0, 0))]
                 + [page_spec(p) for p in range(ppg)] + [page_spec(p) for p in range(ppg)],
        out_specs=pl.BlockSpec((None, steps, width), lambda b, g, pt: (b, 0, 0)),
        scratch_shapes=[pltpu.VMEM((nrow, 1), F32), pltpu.VMEM((nrow, 1), F32), pltpu.VMEM((nrow, dv), F32)],
    )
    return pl.pallas_call(
        kern,
        grid_spec=grid_spec,
        out_shape=jax.ShapeDtypeStruct((bsz, steps, width), F32),
        compiler_params=_cparams(("parallel", "arbitrary")),
        name="attn_sample",
    )(page_table.reshape(-1), lam, q_rows, rows(k_new), rows(v_new), bias_new, bias_past, subln_g.reshape(1, dv),
      *([pages_k] * ppg), *([pages_v] * ppg))


def _heads_kernel(k_ref, v_ref, ko_ref, vo_ref):
    ko_ref[...] = k_ref[...].reshape(ko_ref.shape)
    vo_ref[...] = v_ref[...].reshape(vo_ref.shape)


def _heads_call(proj, kblk, vblk, n_heads):
    t = proj.shape[0]
    dv = 2 * C_DQ
    tm = min(256, t)
    out = jax.ShapeDtypeStruct((t, n_heads, dv), proj.dtype)
    ospec = pl.BlockSpec((tm, n_heads, dv), lambda i: (i, 0, 0))
    return pl.pallas_call(
        _heads_kernel,
        grid=(t // tm,),
        in_specs=[pl.BlockSpec((tm, n_heads * dv), lambda i: (i, kblk)),
                  pl.BlockSpec((tm, n_heads * dv), lambda i: (i, vblk))],
        out_specs=[ospec, ospec],
        out_shape=[out, out],
        compiler_params=_cparams(("parallel",)),
        name="kv_heads",
    )(proj, proj)


def _merge_kernel(ya_ref, yb_ref, yc_ref, ga_ref, gb_ref, gc_ref, w_ref, o_ref):
    acc = None
    for n, (y_ref, g_ref) in enumerate(((ya_ref, ga_ref), (yb_ref, gb_ref), (yc_ref, gc_ref))):
        term = _sigmoid(g_ref[...]) * jnp.dot(y_ref[...].astype(BF16), w_ref[n].astype(BF16),
                                              preferred_element_type=F32)
        acc = term if acc is None else acc + term
    o_ref[...] = acc.astype(o_ref.dtype)


def _merge_call(ya, yb, yc, proj, gate_col0, w_branch, tm, tn):
    t, wdt = ya.shape
    d = w_branch.shape[2]
    nb = d // tn
    y_spec = pl.BlockSpec((tm, wdt), lambda i, j: (i, 0))
    g_spec = lambda n: pl.BlockSpec((tm, tn), lambda i, j: (i, gate_col0 + n * nb + j))
    return pl.pallas_call(
        _merge_kernel,
        grid=(t // tm, nb),
        in_specs=[y_spec, y_spec, y_spec, g_spec(0), g_spec(1), g_spec(2),
                  pl.BlockSpec((N_BRANCH, wdt, tn), lambda i, j: (0, 0, j))],
        out_specs=pl.BlockSpec((tm, tn), lambda i, j: (i, j)),
        out_shape=jax.ShapeDtypeStruct((t, d), BF16),
        compiler_params=_cparams(("parallel", "arbitrary")),
        name="merge",
    )(ya, yb, yc, proj, proj, proj, w_branch)


def _outproj_kernel(m_ref, x_ref, g1_ref, w_ref, o_ref):
    o_ref[...] = x_ref[...] + g1_ref[...] * jnp.dot(m_ref[...], w_ref[...].astype(BF16),
                                                   preferred_element_type=F32)


def _outproj_call(merged, x, grp, layer, gate_chunk, w, tn):
    t, d = x.shape
    tm = grp.tm
    nb = d // tn
    if grp.per_token:
        g1 = pl.BlockSpec((None, tm, tn), lambda i, j: (layer, i, gate_chunk * nb + j))
    else:
        seq = grp.seq
        g1 = pl.BlockSpec((None, None, 1, tn), lambda i, j: (layer, (i * tm) // seq, 0, gate_chunk * nb + j))
    return pl.pallas_call(
        _outproj_kernel,
        grid=(t // tm, nb),
        in_specs=[pl.BlockSpec((tm, d), lambda i, j: (i, 0)),
                  pl.BlockSpec((tm, tn), lambda i, j: (i, j)),
                  g1,
                  pl.BlockSpec((d, tn), lambda i, j: (0, j))],
        out_specs=pl.BlockSpec((tm, tn), lambda i, j: (i, j)),
        out_shape=jax.ShapeDtypeStruct((t, d), F32),
        compiler_params=_cparams(("parallel", "arbitrary")),
        name="outproj",
    )(merged, x, grp.mod_arr, w)


def _first_max(slabs, rowi, n_rows):
    m = functools.reduce(jnp.maximum, [jnp.max(s, axis=0, keepdims=True) for s in slabs])
    cand = [jnp.min(jnp.where(s == m, rowi + g * SUBLANE, n_rows), axis=0, keepdims=True)
            for g, s in enumerate(slabs)]
    return m, functools.reduce(jnp.minimum, cand)


def _router_kernel(x_ref, sc_ref, sh_ref, g_ref, wrt_ref, rb_ref, tri_ref, wsg_ref, wsu_ref, wsd_ref,
                   h_ref, s_ref, idx_ref, w_ref, rank_ref, cnt_ref, *, n_experts):
    h = _rms(x_ref[...]) * g_ref[...]
    h = h * (1.0 + sc_ref[...]) + sh_ref[...]
    h_ref[...] = h.reshape(h_ref.shape)
    hb = h.astype(BF16)
    a = _silu(jnp.dot(hb, wsg_ref[...].astype(BF16), preferred_element_type=F32))
    a = a * jnp.dot(hb, wsu_ref[...].astype(BF16), preferred_element_type=F32)
    s_ref[...] = _dot(a, wsd_ref[...])
    tm = h.shape[0]
    ge = n_experts // N_GROUPS
    scores = _sigmoid(_dot_nt(wrt_ref[...], h))
    sel = scores + rb_ref[...]
    neg = -jnp.inf
    rowi = lax.broadcasted_iota(jnp.int32, (ge, tm), 0)
    slabs = [sel[g * ge:(g + 1) * ge] for g in range(N_GROUPS)]
    sc_slabs = [scores[g * ge:(g + 1) * ge] for g in range(N_GROUPS)]

    gsc = jnp.zeros((N_GROUPS, tm), F32)
    for g, slab in enumerate(slabs):
        m1, i1 = _first_max([slab], rowi, ge)
        m2 = jnp.max(jnp.where(rowi == i1, neg, slab), axis=0, keepdims=True)
        gsc = jnp.where(rowi == g, m1 + m2, gsc)
    gkeep = jnp.zeros((N_GROUPS, tm), F32)
    for _ in range(TOPK_GROUPS):
        _, gi = _first_max([gsc], rowi, N_GROUPS)
        hit = rowi == gi
        gkeep = jnp.where(hit, 1.0, gkeep)
        gsc = jnp.where(hit, neg, gsc)
    slabs = [jnp.where(gkeep[g:g + 1] > 0.0, slab, neg) for g, slab in enumerate(slabs)]

    member = [jnp.zeros((ge, tm), F32) for _ in range(N_GROUPS)]
    idx_rows, w_rows = [], []
    for _ in range(TOP_K):
        _, ei = _first_max(slabs, rowi, n_experts)
        wk = jnp.zeros((1, tm), F32)
        for g in range(N_GROUPS):
            hit = (rowi + g * ge) == ei
            member[g] = jnp.where(hit, 1.0, member[g])
            slabs[g] = jnp.where(hit, neg, slabs[g])
            wk = wk + jnp.sum(jnp.where(hit, sc_slabs[g], 0.0), axis=0, keepdims=True)
        idx_rows.append(ei)
        w_rows.append(wk)
    w_sum = functools.reduce(lambda a, b: a + b, w_rows)

    memb = jnp.concatenate(member, axis=0)
    rank_full = jnp.dot(memb.astype(BF16), tri_ref[...], preferred_element_type=F32)
    cnt_ref[...] = jnp.broadcast_to(jnp.sum(memb, axis=-1, keepdims=True), cnt_ref.shape)
    rowk = lax.broadcasted_iota(jnp.int32, (TOP_K, tm), 0)
    idx_out = jnp.zeros((TOP_K, tm), jnp.int32)
    w_out = jnp.zeros((TOP_K, tm), F32)
    rank_out = jnp.zeros((TOP_K, tm), F32)
    for k in range(TOP_K):
        rk = jnp.zeros((1, tm), F32)
        for g in range(N_GROUPS):
            rk = rk + jnp.sum(jnp.where((rowi + g * ge) == idx_rows[k], rank_full[g * ge:(g + 1) * ge], 0.0),
                              axis=0, keepdims=True)
        idx_out = jnp.where(rowk == k, idx_rows[k], idx_out)
        w_out = jnp.where(rowk == k, w_rows[k] / w_sum * ROUTED_SCALE, w_out)
        rank_out = jnp.where(rowk == k, rk, rank_out)
    idx_ref[...] = idx_out
    w_ref[...] = w_out
    rank_ref[...] = rank_out.astype(jnp.int32)


def _router_call(x, grp, layer, chunks, g, w_router, router_bias, ws_gate, ws_up, ws_down):
    t, d = x.shape
    e = w_router.shape[1]
    ds = ws_gate.shape[1]
    tm = grp.tm
    assert e // N_GROUPS == SUBLANE
    tri = jnp.triu(jnp.ones((tm, tm), BF16), 1)
    kern = functools.partial(_router_kernel, n_experts=e)
    row = pl.BlockSpec((TOP_K, tm), lambda i: (0, i))
    full = lambda a, b: pl.BlockSpec((a, b), lambda i: (0, 0))
    return pl.pallas_call(
        kern,
        grid=(t // tm,),
        in_specs=[pl.BlockSpec((tm, d), lambda i: (i, 0)),
                  grp.mod_spec(layer, chunks[0], d),
                  grp.mod_spec(layer, chunks[1], d),
                  full(1, d), full(e, d), full(e, 1), full(tm, tm), full(d, ds), full(d, ds), full(ds, d)],
        out_specs=[pl.BlockSpec((tm, d // LANE, LANE), lambda i: (i, 0, 0)),
                   pl.BlockSpec((tm, d), lambda i: (i, 0)), row, row, row,
                   pl.BlockSpec((None, e, LANE), lambda i: (i, 0, 0))],
        out_shape=[jax.ShapeDtypeStruct((t, d // LANE, LANE), F32), jax.ShapeDtypeStruct((t, d), F32),
                   jax.ShapeDtypeStruct((TOP_K, t), jnp.int32), jax.ShapeDtypeStruct((TOP_K, t), F32),
                   jax.ShapeDtypeStruct((TOP_K, t), jnp.int32), jax.ShapeDtypeStruct((t // tm, e, LANE), F32)],
        compiler_params=_cparams(("parallel",)),
        name="router",
    )(x, grp.mod_arr, grp.mod_arr, g.reshape(1, d), jnp.transpose(w_router), router_bias.reshape(e, 1).astype(F32),
      tri, ws_gate, ws_up, ws_down)


def _dispatch_kernel(dest_ref, h_ref, xs_hbm, sem, *, tc):
    def issue(r, carry):
        pltpu.make_async_copy(h_ref.at[lax.rem(r, tc)], xs_hbm.at[dest_ref[r]], sem).start()
        return carry

    lax.fori_loop(0, tc * TOP_K, issue, 0)
    for _ in range(TOP_K):
        pltpu.make_async_copy(h_ref, xs_hbm.at[pl.ds(0, tc)], sem).wait()


def _dispatch_call(dest_km, h3, tc):
    t, s, lanes = h3.shape
    kern = functools.partial(_dispatch_kernel, tc=tc)
    return pl.pallas_call(
        kern,
        grid=(t // tc,),
        in_specs=[pl.BlockSpec((None, None, TOP_K * tc), lambda i: (i, 0, 0), memory_space=pltpu.SMEM),
                  pl.BlockSpec((tc, s, lanes), lambda i: (i, 0, 0))],
        out_specs=pl.BlockSpec(memory_space=pl.ANY),
        out_shape=jax.ShapeDtypeStruct((t * TOP_K, s, lanes), F32),
        scratch_shapes=[pltpu.SemaphoreType.DMA(())],
        compiler_params=_cparams(("arbitrary",)),
        name="dispatch",
    )(dest_km, h3)


def _experts_kernel(iblk_ref, ie_ref, ilo_ref, ihi_ref, ilast_ref, x_ref, wg_ref, wu_ref, wd_ref, y_ref,
                    wg_scr, wu_scr, wd_scr, y_scr, *, blk):
    i = pl.program_id(0)
    lo = ilo_ref[i]
    hi = ihi_ref[i]
    prev = ie_ref[jnp.maximum(i - 1, 0)]
    d = y_scr.shape[1]

    @pl.when((i == 0) | (prev != ie_ref[i]))
    def _():
        wg_scr[...] = wg_ref[...].astype(BF16)
        wu_scr[...] = wu_ref[...].astype(BF16)
        wd_scr[...] = wd_ref[...].astype(BF16)

    @pl.when(hi > lo)
    def _():
        xb = x_ref[...].reshape(blk, d).astype(BF16)
        a = _silu(jnp.dot(xb, wg_scr[...], preferred_element_type=F32))
        a = a * jnp.dot(xb, wu_scr[...], preferred_element_type=F32)
        res = jnp.dot(a.astype(BF16), wd_scr[...], preferred_element_type=F32)
        row = lax.broadcasted_iota(jnp.int32, (blk, 1), 0)
        mine = (row >= lo) & (row < hi)

        @pl.when(lo == 0)
        def _():
            y_scr[...] = jnp.where(mine, res, 0.0)

        @pl.when(lo > 0)
        def _():
            y_scr[...] = jnp.where(mine, res, y_scr[...])

    @pl.when(ilast_ref[i] == 1)
    def _():
        y_ref[...] = y_scr[...].reshape(y_ref.shape)


def _experts_call(items, xs, w_gate, w_up, w_down, layer, blk):
    iblk, ie, ilo, ihi, ilast = items
    n_rows, s, lanes = xs.shape
    d = s * lanes
    de = w_gate.shape[3]
    kern = functools.partial(_experts_kernel, blk=blk)
    wspec = lambda a, b: pl.BlockSpec((None, None, a, b), lambda i, ib, ie_, *_: (layer, ie_[i], 0, 0))
    rows = pl.BlockSpec((blk, s, lanes), lambda i, ib, *_: (ib[i], 0, 0))
    grid_spec = pltpu.PrefetchScalarGridSpec(
        num_scalar_prefetch=5,
        grid=(iblk.shape[0],),
        in_specs=[rows, wspec(d, de), wspec(d, de), wspec(de, d)],
        out_specs=rows,
        scratch_shapes=[pltpu.VMEM((d, de), BF16), pltpu.VMEM((d, de), BF16), pltpu.VMEM((de, d), BF16),
                        pltpu.VMEM((blk, d), F32)],
    )
    return pl.pallas_call(
        kern,
        grid_spec=grid_spec,
        out_shape=jax.ShapeDtypeStruct((n_rows, s, lanes), F32),
        compiler_params=_cparams(("arbitrary",)),
        name="experts",
    )(iblk, ie, ilo, ihi, ilast, xs, w_gate, w_up, w_down)


def _combine_kernel(pos_ref, x_ref, sh_ref, g2_ref, fg_ref, w_ref, y_hbm, o_ref, buf, sem, *, tc, final):
    def issue(r, carry):
        pltpu.make_async_copy(y_hbm.at[pos_ref[r]], buf.at[r], sem).start()
        return carry

    lax.fori_loop(0, tc * TOP_K, issue, 0)
    pltpu.make_async_copy(y_hbm.at[pl.ds(0, tc * TOP_K)], buf, sem).wait()
    d = x_ref.shape[1]
    acc = buf[pl.ds(0, tc)].reshape(tc, d) * w_ref[:, 0:1]
    for k in range(1, TOP_K):
        acc = acc + buf[pl.ds(k * tc, tc)].reshape(tc, d) * w_ref[:, k:k + 1]
    out = x_ref[...] + g2_ref[...] * (acc + sh_ref[...])
    o_ref[...] = _rms(out) * fg_ref[...] if final else out


def _combine_call(pos_km, wts, x, shared, grp, layer, gate_chunk, final_g, y_sorted, tc, final):
    t, d = x.shape
    kern = functools.partial(_combine_kernel, tc=tc, final=final)
    seq = grp.seq
    if grp.per_token:
        g2 = pl.BlockSpec((None, tc, d), lambda i: (layer, i, gate_chunk))
    else:
        g2 = pl.BlockSpec((None, None, 1, d), lambda i: (layer, (i * tc) // seq, 0, gate_chunk))
    tile = pl.BlockSpec((tc, d), lambda i: (i, 0))
    return pl.pallas_call(
        kern,
        grid=(t // tc,),
        in_specs=[pl.BlockSpec((None, None, TOP_K * tc), lambda i: (i, 0, 0), memory_space=pltpu.SMEM),
                  tile, tile, g2,
                  pl.BlockSpec((1, d), lambda i: (0, 0)),
                  pl.BlockSpec((tc, TOP_K), lambda i: (i, 0)),
                  pl.BlockSpec(memory_space=pl.ANY)],
        out_specs=tile,
        out_shape=jax.ShapeDtypeStruct((t, d), F32),
        scratch_shapes=[pltpu.VMEM((TOP_K * tc,) + y_sorted.shape[1:], F32), pltpu.SemaphoreType.DMA(())],
        compiler_params=_cparams(("arbitrary",)),
        name="combine",
    )(pos_km, x, shared, grp.mod_arr, final_g.reshape(1, d), wts, y_sorted)


def _route_plan(idx_t, rank_t, counts, tm, blk, tc):
    _, t = idx_t.shape
    n_tiles, n_experts = counts.shape
    n_rows = t * TOP_K
    total = jnp.sum(counts, axis=0)
    e_end = jnp.cumsum(total)
    e_start = e_end - total
    table = e_start[None, :] + jnp.cumsum(counts, axis=0) - counts
    idx4 = idx_t.reshape(TOP_K, n_tiles, tm, 1)
    hit = idx4 == jnp.arange(n_experts, dtype=jnp.int32)
    dest = jnp.sum(jnp.where(hit, table[None, :, None, :], 0), axis=-1).reshape(TOP_K, t) + rank_t
    dest_km = dest.reshape(TOP_K, t // tc, tc).transpose(1, 0, 2).reshape(t // tc, 1, TOP_K * tc)

    n_blocks = n_rows // blk
    b_cut = jnp.arange(n_blocks, dtype=jnp.int32) * blk
    e_cut = e_start[1:].astype(jnp.int32)
    n_items = n_blocks + n_experts - 1
    b_at = (jnp.arange(n_blocks, dtype=jnp.int32)
            + jnp.sum(e_cut[None, :] < b_cut[:, None], axis=1).astype(jnp.int32))
    e_at = (jnp.arange(n_experts - 1, dtype=jnp.int32)
            + jnp.minimum(e_cut // blk + 1, n_blocks).astype(jnp.int32))
    cuts = jnp.zeros((n_items,), jnp.int32).at[b_at].set(b_cut).at[e_at].set(e_cut)
    ends = jnp.concatenate([cuts[1:], jnp.array([n_rows], jnp.int32)])
    iblk = jnp.minimum(cuts // blk, n_blocks - 1).astype(jnp.int32)
    ie = jnp.minimum(jnp.sum(e_end[None, :] <= cuts[:, None], axis=1), n_experts - 1).astype(jnp.int32)
    ilo = (cuts - iblk * blk).astype(jnp.int32)
    ihi = (ends - iblk * blk).astype(jnp.int32)
    ilast = jnp.concatenate([iblk[1:] != iblk[:-1], jnp.array([True])]).astype(jnp.int32)
    return dest_km, (iblk, ie, ilo, ihi, ilast)


def _bucket_bias(dist, rel_bias):
    n = jnp.maximum(dist, 0)
    exact = N_BUCKETS // 2
    nf = jnp.maximum(n, exact).astype(F32)
    large = exact + (jnp.log(nf / exact) / math.log(MAX_DISTANCE / exact) * (N_BUCKETS - exact)).astype(jnp.int32)
    bucket = jnp.where(n < exact, n, jnp.minimum(large, N_BUCKETS - 1))
    return rel_bias[bucket].astype(F32)


def _prompt_bias_table(rel_bias, tq, tk):
    ratio = tq // tk
    assert tk + 1 >= MAX_DISTANCE
    r = jnp.arange(tq, dtype=jnp.int32)[:, None]
    c = jnp.arange(tk, dtype=jnp.int32)[None, :]
    tabs = []
    for b in range(ratio + 2):
        dist = (b - (ratio - 1)) * tk + r - c
        bias = jnp.transpose(_bucket_bias(dist, rel_bias), (2, 0, 1))
        tabs.append(jnp.where((dist >= 0)[None], bias, -jnp.inf))
    return jnp.stack(tabs, axis=1)


def _sample_bias_tables(rel_bias, steps, past_len, page):
    assert page + 1 >= MAX_DISTANCE
    qpos = past_len + jnp.arange(steps, dtype=jnp.int32)
    nrow = C_HEADS * 2 * steps
    tabs = []
    for first in (past_len - 2 * page, past_len - page):
        kpos = first + jnp.arange(page, dtype=jnp.int32)
        b = _bucket_bias(qpos[:, None] - kpos[None, :], rel_bias)
        b = jnp.transpose(b, (2, 0, 1))
        b = jnp.broadcast_to(b[:, None], (C_HEADS, 2, steps, page)).reshape(nrow, page)
        own = (jnp.arange(nrow)[:, None] // (2 * steps)) == jnp.arange(C_HEADS)[None, :]
        tabs.append(jnp.where(own[:, None, :], b[:, :, None], -jnp.inf).reshape(nrow, page * C_HEADS))
    bp = jnp.stack(tabs)
    dn = qpos[:, None] - qpos[None, :]
    bn = jnp.where((dn >= 0)[..., None], _bucket_bias(dn, rel_bias), -jnp.inf)
    bn = jnp.transpose(bn, (2, 0, 1))
    bn = jnp.broadcast_to(bn[:, None], (C_HEADS, 2, steps, steps)).reshape(C_HEADS * 2 * steps, steps)
    return bp, bn


def _trunk(x3, grp, p, past, depth):
    bsz, seq, d = x3.shape
    t = bsz * seq
    is_sample = past is not None
    x = x3.reshape(t, d)
    hw = B_HEADS * B_DK
    wdt = A_GROUPS * LANE
    tm = grp.tm
    blk = 256
    tc = min(128, t)
    new_k, new_v, new_s, new_conv, new_va = [], [], [], [], []
    for l in range(depth):
        proj = _inproj_call(x, grp, l, (1, 0), p['norm_mix_g'][l], p['w_in_main'][l], p['tn_in'])
        proj_ab = _inproj_call(x, grp, l, (1, 0), p['norm_mix_g'][l], p['w_in_ab'][l], LANE)
        proj3 = proj.reshape(bsz, seq, -1)

        c_a = min(seq, A_CHUNK)
        causal = jnp.tril(jnp.ones((c_a, c_a), bool))
        w_s = jnp.where(causal[None], p['spatial_w'][l][:, :c_a, :c_a], 0.0)
        sb = p['spatial_b'][l][:, :c_a]
        if c_a < A_CHUNK:
            rep = A_CHUNK // c_a
            w_s = jax.vmap(lambda m: jnp.kron(jnp.eye(rep, dtype=F32), m))(w_s)
            sb = jnp.tile(sb, (1, rep))
        sb_full = jnp.repeat(jnp.transpose(sb), LANE, axis=1)
        y_a, v_norm = _mixa_call(proj, w_s, sb_full, p['ln_v_g'][l], p['ln_v_b'][l])

        a_b = proj_ab[:, :B_HEADS]
        b_b = proj_ab[:, B_HEADS:2 * B_HEADS]
        beta = jax.nn.sigmoid(b_b).reshape(bsz, seq, B_HEADS)
        g_log = (-jnp.exp(p['A_log'][l].astype(F32))
                 * jax.nn.softplus(a_b + p['dt_bias'][l].astype(F32))).reshape(bsz, seq, B_HEADS)
        qkv_raw = proj3[:, :, 2 * wdt:2 * wdt + 3 * hw]
        if is_sample:
            tr = lambda a: jnp.transpose(a, (1, 0, 2))
            y_b_t, s_new = _gdn_sample_call(tr(qkv_raw), tr(past['state_conv'][l]),
                                            tr(proj3[:, :, 2 * wdt + 3 * hw:2 * wdt + 4 * hw]),
                                            tr(g_log), tr(beta), past['state_gdn'][l], p['conv_w'][l],
                                            p['gdn_norm_g'][l])
            y_b = tr(y_b_t).reshape(t, hw).astype(BF16)
            conv_state = jnp.concatenate([past['state_conv'][l], qkv_raw], axis=1)[:, seq:]
        else:
            c_g = math.gcd(seq, GDN_CHUNK)
            gc_row = jnp.cumsum(jnp.transpose(g_log.reshape(bsz, seq // c_g, c_g, B_HEADS), (0, 1, 3, 2)), axis=-1)
            gc_tok = jnp.transpose(gc_row, (0, 1, 3, 2)).reshape(bsz, seq, B_HEADS)
            y_b3, s_new = _gdn_prompt_call(proj3, 2 * wdt // hw, p['conv_w'][l], gc_tok, beta, gc_row,
                                           p['gdn_norm_g'][l])
            y_b = y_b3.reshape(t, hw)
            conv_state = jnp.concatenate([jnp.zeros((bsz, CONV_W - 1, 3 * hw), F32), qkv_raw], axis=1)[:, seq:]

        c0 = 2 * wdt + 4 * hw
        cw = C_HEADS * 2 * C_DQ
        k_c = proj3[:, :, c0 + cw:c0 + 2 * cw]
        v_c = proj3[:, :, c0 + 2 * cw:c0 + 3 * cw]
        lam_init = 0.8 - 0.6 * math.exp(-0.3 * l)
        lam = (jnp.exp(jnp.sum(p['lambda_q1'][l].astype(F32) * p['lambda_k1'][l].astype(F32)))
               - jnp.exp(jnp.sum(p['lambda_q2'][l].astype(F32) * p['lambda_k2'][l].astype(F32))) + lam_init)
        lam = lam.reshape(1).astype(F32)
        if is_sample:
            y_c = _attn_sample_call(past['page_table'], lam, proj3[:, :, c0:c0 + cw], k_c, v_c,
                                    p['bias_new'], p['bias_past'], p['subln_g'][l], past['cache_k'],
                                    past['cache_v'], l, 1.0 - lam_init, p['ppg'])
            y_c = y_c.reshape(t, cw).astype(BF16)
        else:
            nb = c0 // (2 * C_DQ)
            y_c = _attn_prompt_call(proj3, nb, nb + C_HEADS, nb + 2 * C_HEADS, p['bias_prompt'], lam,
                                    p['subln_g'][l], 1.0 - lam_init, p['tq']).reshape(t, cw)

        gate0 = c0 + 3 * cw
        tn_m = 512
        merged = _merge_call(y_a, y_b, y_c, proj, gate0 // tn_m, p['w_branch'][l], tm, tn_m)
        x = _outproj_call(merged, x, grp, l, 2, p['w_out'][l], 512)

        h3, shared, idx_t, w_t, rank_t, cnt = _router_call(
            x, grp, l, (4, 3), p['norm_moe_g'][l], p['w_router'][l], p['router_bias'][l],
            p['ws_gate'][l], p['ws_up'][l], p['ws_down'][l])
        dest_km, items = _route_plan(idx_t, rank_t, cnt[:, :, 0].astype(jnp.int32), tm, blk, tc)
        xs = _dispatch_call(dest_km, h3, tc)
        y_sorted = _experts_call(items, xs, p['w_gate'], p['w_up'], p['w_down'], l, blk)
        x = _combine_call(dest_km, jnp.transpose(w_t), x, shared, grp, l, 5, p['final_norm_g'], y_sorted, tc,
                          l == depth - 1)

        k_heads, v_heads = _heads_call(proj, (c0 + cw) // cw, (c0 + 2 * cw) // cw, C_HEADS)
        new_k.append(k_heads.reshape(bsz, seq, C_HEADS, 2 * C_DQ))
        new_v.append(v_heads.reshape(bsz, seq, C_HEADS, 2 * C_DQ))
        new_s.append(s_new)
        new_conv.append(conv_state)
        if is_sample:
            new_va.append(v_norm.reshape(bsz, seq, wdt))
    chunk_v = jnp.stack(new_va) if is_sample else None
    return (x.reshape(bsz, seq, d), jnp.stack(new_k), jnp.stack(new_v), jnp.stack(new_s), jnp.stack(new_conv),
            chunk_v)


def kernel(x_prompt, x_sample, cache_k, cache_v, state_gdn, state_conv, page_table, c_prompt, c_sample,
           w_mod, b_mod, norm_mix_g, w_in, spatial_w, spatial_b, ln_v_g, ln_v_b, conv_w, A_log, dt_bias,
           gdn_norm_g, lambda_q1, lambda_k1, lambda_q2, lambda_k2, subln_g, rel_bias, w_branch, w_out,
           norm_moe_g, w_router, router_bias, w_gate, w_up, w_down, ws_gate, ws_up, ws_down, final_norm_g):
    depth = w_mod.shape[0]
    bsz, seq, d = x_prompt.shape
    dbsz, dseq, _ = x_sample.shape
    wdt = A_GROUPS * LANE
    hw = B_HEADS * B_DK
    n_ab = 2 * B_HEADS
    ab0 = 2 * wdt + 4 * hw
    w_in_main = jnp.concatenate([w_in[:, :, :ab0], w_in[:, :, ab0 + n_ab:]], axis=2)
    w_in_ab = jnp.pad(w_in[:, :, ab0:ab0 + n_ab], ((0, 0), (0, 0), (0, LANE - n_ab)))
    n_main = w_in_main.shape[2]
    tn_in = next(c for c in (1536, 1280, 1024, 512, 256, 128) if n_main % c == 0)

    mod = _mod_call(jnp.concatenate([c_prompt, c_sample], axis=0), w_mod, b_mod)
    mod_prompt = mod[:, :bsz].reshape(depth, bsz, 1, mod.shape[2])
    mod_sample = jnp.repeat(mod[:, bsz:], dseq, axis=1)

    past_len = page_table.shape[1] * cache_k.shape[2]
    tq = min(256, seq)
    bias_past, bias_new = _sample_bias_tables(rel_bias, dseq, past_len, cache_k.shape[2])
    p = dict(norm_mix_g=norm_mix_g, w_in_main=w_in_main, w_in_ab=w_in_ab, tn_in=tn_in, spatial_w=spatial_w,
             spatial_b=spatial_b, ln_v_g=ln_v_g, ln_v_b=ln_v_b, conv_w=conv_w, A_log=A_log, dt_bias=dt_bias,
             gdn_norm_g=gdn_norm_g, lambda_q1=lambda_q1, lambda_k1=lambda_k1, lambda_q2=lambda_q2,
             lambda_k2=lambda_k2, subln_g=subln_g, w_branch=w_branch, w_out=w_out, norm_moe_g=norm_moe_g,
             w_router=w_router, router_bias=router_bias, w_gate=w_gate, w_up=w_up, w_down=w_down,
             ws_gate=ws_gate, ws_up=ws_up, ws_down=ws_down, final_norm_g=final_norm_g,
             bias_prompt=_prompt_bias_table(rel_bias, tq, min(LANE, seq)), tq=tq, bias_past=bias_past, bias_new=bias_new,
             ppg=math.gcd(page_table.shape[1], 4))
    past = dict(cache_k=cache_k, cache_v=cache_v,
                state_gdn=state_gdn, state_conv=state_conv, page_table=page_table)

    grp_p = _Group(mod_prompt, False, seq, min(512, seq))
    grp_s = _Group(mod_sample, True, dseq, min(512, dbsz * dseq))
    y_p, k_p, v_p, s_p, conv_p, _ = _trunk(x_prompt, grp_p, p, None, depth)
    y_s, k_s, v_s, s_s, conv_s, chunk_v = _trunk(x_sample, grp_s, p, past, depth)
    return (y_p, y_s, k_p, v_p, k_s, v_s, s_p, s_s, conv_p, conv_s, chunk_v)
```

```python
import functools
import math

import numpy as np
import jax
import jax.numpy as jnp
from jax import lax
from jax.experimental import pallas as pl
from jax.experimental.pallas import tpu as pltpu

F32 = jnp.float32
BF16 = jnp.bfloat16
EPS = 1e-6

LANE = 128
SUBLANE = 8
VMEM_LIMIT_BYTES = 56 * 1024 * 1024

A_GROUPS = 8
A_CHUNK = 128
B_HEADS = 8
B_DK = 128
CONV_W = 4
GDN_CHUNK = 64
C_HEADS = 8
C_DQ = 64
N_BUCKETS = 32
MAX_DISTANCE = 128
TOP_K = 8
N_GROUPS = 8
TOPK_GROUPS = 4
ROUTED_SCALE = 2.5
N_BRANCH = 3
GELU_C = float(np.float32(np.sqrt(2.0 / np.pi)))


def _cparams(sem):
    return pltpu.CompilerParams(dimension_semantics=sem, vmem_limit_bytes=VMEM_LIMIT_BYTES)


def _sigmoid(x):
    return 1.0 / (1.0 + jnp.exp(-x))


def _silu(x):
    return x * _sigmoid(x)


def _gelu(x):
    return x * (0.5 * (1.0 + jnp.tanh(GELU_C * (x + 0.044715 * (x * x * x)))))


def _dot(a, b):
    return jnp.dot(a.astype(BF16), b.astype(BF16), preferred_element_type=F32)


def _dot_nt(a, b):
    return lax.dot_general(a.astype(BF16), b.astype(BF16), (((1,), (1,)), ((), ())),
                           preferred_element_type=F32)


def _dot_tn(a, b):
    return lax.dot_general(a.astype(BF16), b.astype(BF16), (((0,), (0,)), ((), ())),
                           preferred_element_type=F32)


def _split2(a):
    hi = a.astype(BF16)
    return hi, (a - hi.astype(F32)).astype(BF16)


def _dot_hi(a, b):
    a0, a1 = _split2(a)
    b0, b1 = _split2(b)
    d = lambda x, y: jnp.dot(x, y, preferred_element_type=F32)
    return (d(a1, b0) + d(a0, b1)) + d(a0, b0)


def _rms(x):
    return x * lax.rsqrt(jnp.mean(x * x, axis=-1, keepdims=True) + EPS)


def _mod_kernel(c_ref, w_ref, b_ref, o_ref):
    o_ref[...] = _dot(_silu(c_ref[...]), w_ref[...]) + b_ref[...]


def _mod_call(c_all, w_mod, b_mod):
    depth, d, n = w_mod.shape
    m = c_all.shape[0]
    tn = 1536 if n % 1536 == 0 else n
    return pl.pallas_call(
        _mod_kernel,
        grid=(depth, n // tn),
        in_specs=[pl.BlockSpec((m, d), lambda l, j: (0, 0)),
                  pl.BlockSpec((None, d, tn), lambda l, j: (l, 0, j)),
                  pl.BlockSpec((None, 1, tn), lambda l, j: (l, 0, j))],
        out_specs=pl.BlockSpec((None, m, tn), lambda l, j: (l, 0, j)),
        out_shape=jax.ShapeDtypeStruct((depth, m, n), F32),
        compiler_params=_cparams(("parallel", "parallel")),
        name="mod",
    )(c_all, w_mod, b_mod.reshape(depth, 1, n))


class _Group:
    def __init__(self, mod_arr, per_token, seq, tm):
        self.mod_arr = mod_arr
        self.per_token = per_token
        self.seq = seq
        self.tm = tm

    def mod_spec(self, layer, chunk, d, tm=None):
        tm, seq = tm or self.tm, self.seq
        if self.per_token:
            return pl.BlockSpec((None, tm, d), lambda i, *_: (layer, i, chunk))
        return pl.BlockSpec((None, None, 1, d), lambda i, *_: (layer, (i * tm) // seq, 0, chunk))


def _inproj_kernel(x_ref, sc_ref, sh_ref, g_ref, w_ref, o_ref, h_scr):
    @pl.when(pl.program_id(1) == 0)
    def _():
        h = _rms(x_ref[...]) * g_ref[...]
        h_scr[...] = (h * (1.0 + sc_ref[...]) + sh_ref[...]).astype(BF16)

    o_ref[...] = jnp.dot(h_scr[...], w_ref[...].astype(BF16), preferred_element_type=F32)


def _inproj_call(x, grp, layer, chunks, g, w, tn):
    t, d = x.shape
    n = w.shape[1]
    tm = grp.tm
    if not grp.per_token and grp.seq % (2 * tm) == 0 and n % 1024 == 0 and tn > 1024:
        tm, tn = 2 * tm, 1024
    return pl.pallas_call(
        _inproj_kernel,
        grid=(t // tm, n // tn),
        in_specs=[pl.BlockSpec((tm, d), lambda i, j: (i, 0)),
                  grp.mod_spec(layer, chunks[0], d, tm),
                  grp.mod_spec(layer, chunks[1], d, tm),
                  pl.BlockSpec((1, d), lambda i, j: (0, 0)),
                  pl.BlockSpec((d, tn), lambda i, j: (0, j))],
        out_specs=pl.BlockSpec((tm, tn), lambda i, j: (i, j)),
        out_shape=jax.ShapeDtypeStruct((t, n), F32),
        scratch_shapes=[pltpu.VMEM((tm, d), BF16)],
        compiler_params=_cparams(("parallel", "arbitrary")),
        name="inproj",
    )(x, grp.mod_arr, grp.mod_arr, g.reshape(1, d), w)


def _mixa_kernel(u_ref, v_ref, w_ref, sb_ref, lg_ref, lb_ref, y_ref, vn_ref):
    v = _gelu(v_ref[...])
    xc = v - jnp.mean(v, axis=-1, keepdims=True)
    vn = xc * lax.rsqrt(jnp.mean(xc * xc, axis=-1, keepdims=True) + EPS) * lg_ref[...] + lb_ref[...]
    vn_ref[...] = vn
    vb = vn.astype(BF16)
    gd = LANE
    for g in range(A_GROUPS):
        sl = slice(g * gd, (g + 1) * gd)
        s = jnp.dot(w_ref[g].astype(BF16), vb[:, sl], preferred_element_type=F32) + sb_ref[:, sl]
        y_ref[:, sl] = (_gelu(u_ref[:, sl]) * s).astype(y_ref.dtype)


def _mixa_call(proj, w_chunk, sb_full, ln_g, ln_b):
    t = proj.shape[0]
    wdt = A_GROUPS * LANE
    c = A_CHUNK
    return pl.pallas_call(
        _mixa_kernel,
        grid=(t // c,),
        in_specs=[pl.BlockSpec((c, wdt), lambda i: (i, 0)),
                  pl.BlockSpec((c, wdt), lambda i: (i, 1)),
                  pl.BlockSpec((A_GROUPS, c, c), lambda i: (0, 0, 0)),
                  pl.BlockSpec((c, wdt), lambda i: (0, 0)),
                  pl.BlockSpec((1, wdt), lambda i: (0, 0)),
                  pl.BlockSpec((1, wdt), lambda i: (0, 0))],
        out_specs=[pl.BlockSpec((c, wdt), lambda i: (i, 0)),
                   pl.BlockSpec((c, wdt), lambda i: (i, 0))],
        out_shape=[jax.ShapeDtypeStruct((t, wdt), BF16),
                   jax.ShapeDtypeStruct((t, wdt), F32)],
        compiler_params=_cparams(("parallel",)),
        name="mixer_a",
    )(proj, proj, w_chunk, sb_full, ln_g.reshape(1, wdt), ln_b.reshape(1, wdt))


def _gdn_prompt_kernel(q_ref, k_ref, v_ref, z_ref, cw_ref, gct_ref, bt_ref, gcr_ref, ng_ref,
                       y_ref, s_ref, xp_scr, s_scr, *, chunk, n_chunks):
    c = chunk
    hw = B_HEADS * B_DK
    step = pl.program_id(1)

    @pl.when(step == 0)
    def _():
        s_scr[...] = jnp.zeros_like(s_scr)
        xp_scr[:, 0:SUBLANE, :] = jnp.zeros((3, SUBLANE, hw), F32)

    act = []
    for idx, r in enumerate((q_ref, k_ref, v_ref)):
        xp_scr[idx, SUBLANE:SUBLANE + c, :] = r[...]
        a = None
        for i in range(CONV_W):
            off = SUBLANE - (CONV_W - 1) + i
            term = xp_scr[idx, off:off + c, :] * cw_ref[i:i + 1, idx * hw:(idx + 1) * hw]
            a = term if a is None else a + term
        xp_scr[idx, 0:SUBLANE, :] = xp_scr[idx, c:c + SUBLANE, :]
        act.append(_silu(a))
    qa, ka, va = act

    gct = gct_ref[...]
    bt = bt_ref[...]
    gcr_all = gcr_ref[...]
    row = lax.broadcasted_iota(jnp.int32, (c, c), 0)
    col = lax.broadcasted_iota(jnp.int32, (c, c), 1)
    causal = row >= col
    strict = row > col
    levels = int(math.log2(c))
    ng = ng_ref[...]

    hs = range(B_HEADS)
    heads = lambda a: jnp.stack([a[:, h * B_DK:(h + 1) * B_DK] for h in hs], axis=0)
    bdot = lambda a, b: jnp.einsum('hij,hjk->hik', a.astype(BF16), b.astype(BF16), preferred_element_type=F32)

    def bdot_hi(a, b):
        a0, a1 = _split2(a)
        b0, b1 = _split2(b)
        d = lambda x, y: jnp.einsum('hij,hjk->hik', x, y, preferred_element_type=F32)
        return (d(a1, b0) + d(a0, b1)) + d(a0, b0)

    q = heads(qa)
    q = q * lax.rsqrt(jnp.sum(q * q, axis=-1, keepdims=True) + EPS) * (B_DK ** -0.5)
    k = heads(ka)
    k = k * lax.rsqrt(jnp.sum(k * k, axis=-1, keepdims=True) + EPS)
    v = heads(va)
    gcc = jnp.stack([gct[:, h:h + 1] for h in hs], axis=0)
    bc = jnp.stack([bt[:, h:h + 1] for h in hs], axis=0)
    gcr = jnp.stack([gcr_all[h:h + 1, :] for h in hs], axis=0)
    decay = jnp.exp(jnp.where(causal[None], gcc - gcr, -jnp.inf))
    qkk = jnp.einsum('hid,hjd->hij', jnp.concatenate([q, k], axis=1).astype(BF16), k.astype(BF16),
                     preferred_element_type=F32)
    qk = qkk[:, :c] * decay
    x = -jnp.where(strict[None], qkk[:, c:] * decay * bc, 0.0)
    r_acc = x
    p = x
    for _ in range(levels - 1):
        p = bdot_hi(p, p)
        r_acc = r_acc + p + bdot_hi(r_acc, p)
    egc = jnp.exp(gcc)
    rhs = jnp.concatenate([v * bc, k * (bc * egc)], axis=2)
    uw = rhs + bdot_hi(r_acc, rhs)
    u = uw[:, :, :B_DK]
    w = uw[:, :, B_DK:]
    s_old = s_scr[...]
    wq = bdot(jnp.concatenate([w, q * egc], axis=1), s_old)
    v_new = u - wq[:, :c]
    o = wq[:, c:] + bdot(qk, v_new)
    g_last = gcr[:, :, c - 1:c]
    kd = k * jnp.exp(g_last - gcc)
    upd = jnp.stack([_dot_tn(kd[h], v_new[h]) for h in hs], axis=0)
    s_scr[...] = s_old * jnp.exp(g_last) + upd
    on = _rms(o) * ng
    for h in hs:
        sl = slice(h * B_DK, (h + 1) * B_DK)
        y_ref[:, sl] = (on[h] * _silu(z_ref[:, sl])).astype(y_ref.dtype)

    @pl.when(step == n_chunks - 1)
    def _():
        s_ref[...] = s_scr[...]


def _gdn_prompt_call(proj3, col0, conv_w, gc_tok, beta_tok, gc_row, norm_g):
    bsz, seq, _ = proj3.shape
    c = math.gcd(seq, GDN_CHUNK)
    n = seq // c
    hw = B_HEADS * B_DK
    kern = functools.partial(_gdn_prompt_kernel, chunk=c, n_chunks=n)
    tok = lambda j: pl.BlockSpec((None, c, hw), lambda b, s: (b, s, col0 + j))
    return pl.pallas_call(
        kern,
        grid=(bsz, n),
        in_specs=[tok(0), tok(1), tok(2), tok(3),
                  pl.BlockSpec((CONV_W, 3 * hw), lambda b, s: (0, 0)),
                  pl.BlockSpec((None, c, B_HEADS), lambda b, s: (b, s, 0)),
                  pl.BlockSpec((None, c, B_HEADS), lambda b, s: (b, s, 0)),
                  pl.BlockSpec((None, None, B_HEADS, c), lambda b, s: (b, s, 0, 0)),
                  pl.BlockSpec((1, B_DK), lambda b, s: (0, 0))],
        out_specs=[pl.BlockSpec((None, c, hw), lambda b, s: (b, s, 0)),
                   pl.BlockSpec((None, B_HEADS, B_DK, B_DK), lambda b, s: (b, 0, 0, 0))],
        out_shape=[jax.ShapeDtypeStruct((bsz, seq, hw), BF16),
                   jax.ShapeDtypeStruct((bsz, B_HEADS, B_DK, B_DK), F32)],
        scratch_shapes=[pltpu.VMEM((3, c + SUBLANE, hw), F32),
                        pltpu.VMEM((B_HEADS, B_DK, B_DK), F32)],
        compiler_params=_cparams(("parallel", "arbitrary")),
        name="gdn_prompt",
    )(proj3, proj3, proj3, proj3, conv_w, gc_tok, beta_tok, gc_row, norm_g.reshape(1, B_DK))


def _gdn_sample_kernel(x_ref, p_ref, z_ref, g_ref, b_ref, s0_ref, cw_ref, ng_ref,
                       y_ref, s_ref, act_scr, o_scr, *, steps, bb):
    hw = B_HEADS * B_DK
    xp = [p_ref[i] for i in range(CONV_W - 1)] + [x_ref[t] for t in range(steps)]
    for t in range(steps):
        a = None
        for i in range(CONV_W):
            term = xp[t + i] * cw_ref[i:i + 1, :]
            a = term if a is None else a + term
        act_scr[t] = _silu(a)

    def head(h, carry):
        off = pl.multiple_of(h * B_DK, B_DK)
        lane_h = lax.broadcasted_iota(jnp.int32, (bb, B_HEADS), 1) == h
        q_t, k_t, v_r, a_c, b_c = [], [], [], [], []
        for t in range(steps):
            q = act_scr[t, :, pl.ds(off, B_DK)]
            q = q * lax.rsqrt(jnp.sum(q * q, axis=-1, keepdims=True) + EPS) * (B_DK ** -0.5)
            k = act_scr[t, :, pl.ds(hw + off, B_DK)]
            k = k * lax.rsqrt(jnp.sum(k * k, axis=-1, keepdims=True) + EPS)
            q_t.append(q.T)
            k_t.append(k.T)
            v_r.append(act_scr[t, :, pl.ds(2 * hw + off, B_DK)])
            a_c.append(jnp.exp(jnp.sum(jnp.where(lane_h, g_ref[t], 0.0), axis=-1, keepdims=True)))
            b_c.append(jnp.sum(jnp.where(lane_h, b_ref[t], 0.0), axis=-1, keepdims=True))
        for b in range(bb):
            s = s0_ref[b, h]
            for t in range(steps):
                a = a_c[t][b:b + 1, :]
                be = b_c[t][b:b + 1, :]
                kc = k_t[t][:, b:b + 1]
                r = jnp.sum(kc * s, axis=0, keepdims=True)
                d = be * (v_r[t][b:b + 1, :] - a * r)
                s = a * s + kc * d
                o_scr[t, b:b + 1, pl.ds(off, B_DK)] = jnp.sum(q_t[t][:, b:b + 1] * s, axis=0, keepdims=True)
            s_ref[b, h] = s
        return carry

    lax.fori_loop(0, B_HEADS, head, 0)

    ng = ng_ref[...]
    for t in range(steps):
        for h in range(B_HEADS):
            sl = slice(h * B_DK, (h + 1) * B_DK)
            y_ref[t, :, sl] = _rms(o_scr[t, :, sl]) * ng * _silu(z_ref[t, :, sl])


def _gdn_sample_call(x_t, prev_t, z_t, g_t, beta_t, s0, conv_w, norm_g):
    steps, bsz, _ = x_t.shape
    hw = B_HEADS * B_DK
    bb = SUBLANE
    kern = functools.partial(_gdn_sample_kernel, steps=steps, bb=bb)
    return pl.pallas_call(
        kern,
        grid=(bsz // bb,),
        in_specs=[pl.BlockSpec((steps, bb, 3 * hw), lambda i: (0, i, 0)),
                  pl.BlockSpec((CONV_W - 1, bb, 3 * hw), lambda i: (0, i, 0)),
                  pl.BlockSpec((steps, bb, hw), lambda i: (0, i, 0)),
                  pl.BlockSpec((steps, bb, B_HEADS), lambda i: (0, i, 0)),
                  pl.BlockSpec((steps, bb, B_HEADS), lambda i: (0, i, 0)),
                  pl.BlockSpec((bb, B_HEADS, B_DK, B_DK), lambda i: (i, 0, 0, 0)),
                  pl.BlockSpec((CONV_W, 3 * hw), lambda i: (0, 0)),
                  pl.BlockSpec((1, B_DK), lambda i: (0, 0))],
        out_specs=[pl.BlockSpec((steps, bb, hw), lambda i: (0, i, 0)),
                   pl.BlockSpec((bb, B_HEADS, B_DK, B_DK), lambda i: (i, 0, 0, 0))],
        out_shape=[jax.ShapeDtypeStruct((steps, bsz, hw), F32),
                   jax.ShapeDtypeStruct((bsz, B_HEADS, B_DK, B_DK), F32)],
        scratch_shapes=[pltpu.VMEM((steps, bb, 3 * hw), F32),
                        pltpu.VMEM((steps, bb, hw), F32)],
        compiler_params=_cparams(("parallel",)),
        name="gdn_sample",
    )(x_t, prev_t, z_t, g_t, beta_t, s0, conv_w, norm_g.reshape(1, B_DK))


def _attn_prompt_kernel(lam_ref, q_ref, k_ref, v_ref, bias_ref, sg_ref, y_ref, m_scr, l_scr, acc_scr,
                        *, tq, tk, out_scale):
    qi = pl.program_id(2)
    dv = 2 * C_DQ
    q = q_ref[...] * (C_DQ ** -0.5)
    lane = lax.broadcasted_iota(jnp.int32, (tq, dv), 1)
    q2 = jnp.concatenate([jnp.where(lane < C_DQ, q, 0.0), jnp.where(lane >= C_DQ, q, 0.0)], axis=0).astype(BF16)
    ratio = tq // tk

    def scores(j):
        start = pl.multiple_of(j * tk, tk)
        bias = bias_ref[jnp.minimum(qi * ratio - j + (ratio - 1), ratio + 1)]
        return _dot_nt(q2, k_ref[pl.ds(start, tk), :]) + jnp.concatenate([bias, bias], axis=0)

    m_scr[...] = jnp.full_like(m_scr, -jnp.inf)

    def sweep_max(jj, carry):
        m = m_scr[...]
        for u in range(ratio):
            m = jnp.maximum(m, scores(jj * ratio + u))
        m_scr[...] = m
        return carry

    lax.fori_loop(0, qi + 1, sweep_max, 0)
    m_scr[...] = jnp.broadcast_to(jnp.max(m_scr[...], axis=-1, keepdims=True), m_scr.shape)
    l_scr[...] = jnp.zeros_like(l_scr)

    def sweep_sum(jj, carry):
        m = m_scr[...]
        l = l_scr[...]
        for u in range(ratio):
            l = l + jnp.exp(scores(jj * ratio + u) - m)
        l_scr[...] = l
        return carry

    lax.fori_loop(0, qi + 1, sweep_sum, 0)
    l_scr[...] = jnp.broadcast_to(1.0 / jnp.sum(l_scr[...], axis=-1, keepdims=True), l_scr.shape)
    acc_scr[...] = jnp.zeros_like(acc_scr)
    lam = lam_ref[0]

    def sweep_values(jj, carry):
        acc = acc_scr[...]
        for u in range(ratio):
            j = jj * ratio + u
            p = jnp.exp(scores(j) - m_scr[...]) * l_scr[...]
            start = pl.multiple_of(j * tk, tk)
            acc = acc + _dot(p[:tq] - lam * p[tq:], v_ref[pl.ds(start, tk), :])
        acc_scr[...] = acc
        return carry

    lax.fori_loop(0, qi + 1, sweep_values, 0)
    y_ref[...] = (_rms(acc_scr[...]) * sg_ref[...] * out_scale).astype(y_ref.dtype)


def _attn_prompt_call(proj3, qcol, kcol, vcol, bias_tab, lam, subln_g, out_scale, tq):
    bsz, seq, _ = proj3.shape
    dv = 2 * C_DQ
    tk = bias_tab.shape[3]
    kern = functools.partial(_attn_prompt_kernel, tq=tq, tk=tk, out_scale=out_scale)
    return pl.pallas_call(
        kern,
        grid=(bsz, C_HEADS, seq // tq),
        in_specs=[pl.BlockSpec(memory_space=pltpu.SMEM),
                  pl.BlockSpec((None, tq, dv), lambda b, h, i: (b, i, qcol + h)),
                  pl.BlockSpec((None, seq, dv), lambda b, h, i: (b, 0, kcol + h)),
                  pl.BlockSpec((None, seq, dv), lambda b, h, i: (b, 0, vcol + h)),
                  pl.BlockSpec((None,) + bias_tab.shape[1:], lambda b, h, i: (h, 0, 0, 0)),
                  pl.BlockSpec((1, dv), lambda b, h, i: (0, 0))],
        out_specs=pl.BlockSpec((None, tq, dv), lambda b, h, i: (b, i, h)),
        out_shape=jax.ShapeDtypeStruct((bsz, seq, C_HEADS * dv), BF16),
        scratch_shapes=[pltpu.VMEM((2 * tq, tk), F32), pltpu.VMEM((2 * tq, tk), F32),
                        pltpu.VMEM((tq, dv), F32)],
        compiler_params=_cparams(("parallel", "parallel", "arbitrary")),
        name="attn_prompt",
    )(lam, proj3, proj3, proj3, bias_tab, subln_g.reshape(1, dv))


def _attn_sample_kernel(pt_ref, lam_ref, q_ref, kn_ref, vn_ref, bnew_ref, bias_ref, sg_ref, *rest,
                        steps, ppg, n_groups, out_scale):
    k_refs = rest[:ppg]
    v_refs = rest[ppg:2 * ppg]
    y_ref, m_scr, l_scr, acc_scr = rest[2 * ppg:]
    g = pl.program_id(1)
    dv = 2 * C_DQ
    nrow = C_HEADS * 2 * steps
    rr = lax.broadcasted_iota(jnp.int32, (nrow, dv), 0)
    cc = lax.broadcasted_iota(jnp.int32, (nrow, dv), 1)
    q = jnp.where(cc // C_DQ == (rr // steps) % 2, q_ref[...] * (C_DQ ** -0.5), 0.0)

    @pl.when(g == 0)
    def _():
        s_cols = [jnp.sum(q * kn_ref[j], axis=-1, keepdims=True) + bnew_ref[:, j:j + 1] for j in range(steps)]
        m = functools.reduce(jnp.maximum, s_cols)
        p_cols = [jnp.exp(sc - m) for sc in s_cols]
        m_scr[...] = m
        l_scr[...] = functools.reduce(lambda a, b: a + b, p_cols)
        acc = p_cols[0] * vn_ref[0]
        for j in range(1, steps):
            acc = acc + p_cols[j] * vn_ref[j]
        acc_scr[...] = acc

    qb = q.astype(BF16)
    s_pages = []
    for pidx in range(ppg):
        is_last = jnp.logical_and(g == n_groups - 1, pidx == ppg - 1).astype(jnp.int32)
        s_pages.append(_dot_nt(qb, k_refs[pidx][...]) + bias_ref[is_last])
    m_old = m_scr[...]
    m_new = functools.reduce(jnp.maximum, [m_old] + [jnp.max(s, axis=-1, keepdims=True) for s in s_pages])
    alpha = jnp.exp(m_old - m_new)
    l_new = alpha * l_scr[...]
    acc = alpha * acc_scr[...]
    for pidx in range(ppg):
        p = jnp.exp(s_pages[pidx] - m_new)
        l_new = l_new + jnp.sum(p, axis=-1, keepdims=True)
        acc = acc + _dot(p, v_refs[pidx][...])
    l_scr[...] = l_new
    acc_scr[...] = acc
    m_scr[...] = m_new

    @pl.when(g == n_groups - 1)
    def _():
        o = acc_scr[...] / l_scr[...]
        lam = lam_ref[0]
        for h in range(C_HEADS):
            r0 = h * 2 * steps
            od = o[r0:r0 + steps] - lam * o[r0 + steps:r0 + 2 * steps]
            y_ref[:, h * dv:(h + 1) * dv] = _rms(od) * sg_ref[...] * out_scale


def _attn_sample_call(page_table, lam, q, k_new, v_new, bias_new, bias_past, subln_g, cache_k, cache_v,
                      layer, out_scale, ppg):
    bsz, steps, width = q.shape
    n_pages = page_table.shape[1]
    depth, n_pool, page, n_heads, dv = cache_k.shape
    n_groups = n_pages // ppg
    nrow = n_heads * 2 * steps
    kern = functools.partial(_attn_sample_kernel, steps=steps, ppg=ppg, n_groups=n_groups, out_scale=out_scale)

    def rows(a):
        a = a.reshape(bsz, steps, n_heads, 1, dv)
        return jnp.broadcast_to(a, (bsz, steps, n_heads, 2 * steps, dv)).reshape(bsz, steps, nrow, dv)

    q_rows = jnp.transpose(q.reshape(bsz, steps, n_heads, dv), (0, 2, 1, 3))
    q_rows = jnp.broadcast_to(q_rows[:, :, None], (bsz, n_heads, 2, steps, dv)).reshape(bsz, nrow, dv)
    pages_k = cache_k.reshape(depth, n_pool, page * n_heads, dv)
    pages_v = cache_v.reshape(depth, n_pool, page * n_heads, dv)

    def page_spec(pidx):
        return pl.BlockSpec((None, None, page * n_heads, dv),
                            lambda b, g, pt: (layer, pt[b * n_pages + g * ppg + pidx], 0, 0))

    new_spec = pl.BlockSpec((None, steps, nrow, dv), lambda b, g, pt: (b, 0, 0, 0))
    grid_spec = pltpu.PrefetchScalarGridSpec(
        num_scalar_prefetch=1,
        grid=(bsz, n_groups),
        in_specs=[pl.BlockSpec(memory_space=pltpu.SMEM),
                  pl.BlockSpec((None, nrow, dv), lambda b, g, pt: (b, 0, 0)), new_spec, new_spec,
                  pl.BlockSpec((nrow, steps), lambda b, g, pt: (0, 0)),
                  pl.BlockSpec((2, nrow, page * n_heads), lambda b, g, pt: (0, 0, 0)),
                  pl.BlockSpec((1, dv), lambda b, g, pt: (0, 0))]
                 + [page_spec(p) for p in range(ppg)] + [page_spec(p) for p in range(ppg)],
        out_specs=pl.BlockSpec((None, steps, width), lambda b, g, pt: (b, 0, 0)),
        scratch_shapes=[pltpu.VMEM((nrow, 1), F32), pltpu.VMEM((nrow, 1), F32), pltpu.VMEM((nrow, dv), F32)],
    )
    return pl.pallas_call(
        kern,
        grid_spec=grid_spec,
        out_shape=jax.ShapeDtypeStruct((bsz, steps, width), F32),
        compiler_params=_cparams(("parallel", "arbitrary")),
        name="attn_sample",
    )(page_table.reshape(-1), lam, q_rows, rows(k_new), rows(v_new), bias_new, bias_past, subln_g.reshape(1, dv),
      *([pages_k] * ppg), *([pages_v] * ppg))


def _heads_kernel(k_ref, v_ref, ko_ref, vo_ref):
    ko_ref[...] = k_ref[...].reshape(ko_ref.shape)
    vo_ref[...] = v_ref[...].reshape(vo_ref.shape)


def _heads_call(proj, kblk, vblk, n_heads):
    t = proj.shape[0]
    dv = 2 * C_DQ
    tm = min(256, t)
    out = jax.ShapeDtypeStruct((t, n_heads, dv), proj.dtype)
    ospec = pl.BlockSpec((tm, n_heads, dv), lambda i: (i, 0, 0))
    return pl.pallas_call(
        _heads_kernel,
        grid=(t // tm,),
        in_specs=[pl.BlockSpec((tm, n_heads * dv), lambda i: (i, kblk)),
                  pl.BlockSpec((tm, n_heads * dv), lambda i: (i, vblk))],
        out_specs=[ospec, ospec],
        out_shape=[out, out],
        compiler_params=_cparams(("parallel",)),
        name="kv_heads",
    )(proj, proj)


def _merge_kernel(ya_ref, yb_ref, yc_ref, ga_ref, gb_ref, gc_ref, w_ref, o_ref):
    acc = None
    for n, (y_ref, g_ref) in enumerate(((ya_ref, ga_ref), (yb_ref, gb_ref), (yc_ref, gc_ref))):
        term = _sigmoid(g_ref[...]) * jnp.dot(y_ref[...].astype(BF16), w_ref[n].astype(BF16),
                                              preferred_element_type=F32)
        acc = term if acc is None else acc + term
    o_ref[...] = acc.astype(o_ref.dtype)


def _merge_call(ya, yb, yc, proj, gate_col0, w_branch, tm, tn):
    t, wdt = ya.shape
    d = w_branch.shape[2]
    nb = d // tn
    y_spec = pl.BlockSpec((tm, wdt), lambda i, j: (i, 0))
    g_spec = lambda n: pl.BlockSpec((tm, tn), lambda i, j: (i, gate_col0 + n * nb + j))
    return pl.pallas_call(
        _merge_kernel,
        grid=(t // tm, nb),
        in_specs=[y_spec, y_spec, y_spec, g_spec(0), g_spec(1), g_spec(2),
                  pl.BlockSpec((N_BRANCH, wdt, tn), lambda i, j: (0, 0, j))],
        out_specs=pl.BlockSpec((tm, tn), lambda i, j: (i, j)),
        out_shape=jax.ShapeDtypeStruct((t, d), BF16),
        compiler_params=_cparams(("parallel", "arbitrary")),
        name="merge",
    )(ya, yb, yc, proj, proj, proj, w_branch)


def _outproj_kernel(m_ref, x_ref, g1_ref, w_ref, o_ref):
    o_ref[...] = x_ref[...] + g1_ref[...] * jnp.dot(m_ref[...], w_ref[...].astype(BF16),
                                                   preferred_element_type=F32)


def _outproj_call(merged, x, grp, layer, gate_chunk, w, tn):
    t, d = x.shape
    tm = grp.tm
    nb = d // tn
    if grp.per_token:
        g1 = pl.BlockSpec((None, tm, tn), lambda i, j: (layer, i, gate_chunk * nb + j))
    else:
        seq = grp.seq
        g1 = pl.BlockSpec((None, None, 1, tn), lambda i, j: (layer, (i * tm) // seq, 0, gate_chunk * nb + j))
    return pl.pallas_call(
        _outproj_kernel,
        grid=(t // tm, nb),
        in_specs=[pl.BlockSpec((tm, d), lambda i, j: (i, 0)),
                  pl.BlockSpec((tm, tn), lambda i, j: (i, j)),
                  g1,
                  pl.BlockSpec((d, tn), lambda i, j: (0, j))],
        out_specs=pl.BlockSpec((tm, tn), lambda i, j: (i, j)),
        out_shape=jax.ShapeDtypeStruct((t, d), F32),
        compiler_params=_cparams(("parallel", "arbitrary")),
        name="outproj",
    )(merged, x, grp.mod_arr, w)


def _first_max(slabs, rowi, n_rows):
    m = functools.reduce(jnp.maximum, [jnp.max(s, axis=0, keepdims=True) for s in slabs])
    cand = [jnp.min(jnp.where(s == m, rowi + g * SUBLANE, n_rows), axis=0, keepdims=True)
            for g, s in enumerate(slabs)]
    return m, functools.reduce(jnp.minimum, cand)


def _router_kernel(x_ref, sc_ref, sh_ref, g_ref, wrt_ref, rb_ref, tri_ref, wsg_ref, wsu_ref, wsd_ref,
                   h_ref, s_ref, idx_ref, w_ref, rank_ref, cnt_ref, *, n_experts):
    h = _rms(x_ref[...]) * g_ref[...]
    h = h * (1.0 + sc_ref[...]) + sh_ref[...]
    h_ref[...] = h.reshape(h_ref.shape)
    hb = h.astype(BF16)
    a = _silu(jnp.dot(hb, wsg_ref[...].astype(BF16), preferred_element_type=F32))
    a = a * jnp.dot(hb, wsu_ref[...].astype(BF16), preferred_element_type=F32)
    s_ref[...] = _dot(a, wsd_ref[...])
    tm = h.shape[0]
    ge = n_experts // N_GROUPS
    scores = _sigmoid(_dot_nt(wrt_ref[...], h))
    sel = scores + rb_ref[...]
    neg = -jnp.inf
    rowi = lax.broadcasted_iota(jnp.int32, (ge, tm), 0)
    slabs = [sel[g * ge:(g + 1) * ge] for g in range(N_GROUPS)]
    sc_slabs = [scores[g * ge:(g + 1) * ge] for g in range(N_GROUPS)]

    gsc = jnp.zeros((N_GROUPS, tm), F32)
    for g, slab in enumerate(slabs):
        m1, i1 = _first_max([slab], rowi, ge)
        m2 = jnp.max(jnp.where(rowi == i1, neg, slab), axis=0, keepdims=True)
        gsc = jnp.where(rowi == g, m1 + m2, gsc)
    gkeep = jnp.zeros((N_GROUPS, tm), F32)
    for _ in range(TOPK_GROUPS):
        _, gi = _first_max([gsc], rowi, N_GROUPS)
        hit = rowi == gi
        gkeep = jnp.where(hit, 1.0, gkeep)
        gsc = jnp.where(hit, neg, gsc)
    slabs = [jnp.where(gkeep[g:g + 1] > 0.0, slab, neg) for g, slab in enumerate(slabs)]

    member = [jnp.zeros((ge, tm), F32) for _ in range(N_GROUPS)]
    idx_rows, w_rows = [], []
    for _ in range(TOP_K):
        _, ei = _first_max(slabs, rowi, n_experts)
        wk = jnp.zeros((1, tm), F32)
        for g in range(N_GROUPS):
            hit = (rowi + g * ge) == ei
            member[g] = jnp.where(hit, 1.0, member[g])
            slabs[g] = jnp.where(hit, neg, slabs[g])
            wk = wk + jnp.sum(jnp.where(hit, sc_slabs[g], 0.0), axis=0, keepdims=True)
        idx_rows.append(ei)
        w_rows.append(wk)
    w_sum = functools.reduce(lambda a, b: a + b, w_rows)

    memb = jnp.concatenate(member, axis=0)
    rank_full = jnp.dot(memb.astype(BF16), tri_ref[...], preferred_element_type=F32)
    cnt_ref[...] = jnp.broadcast_to(jnp.sum(memb, axis=-1, keepdims=True), cnt_ref.shape)
    rowk = lax.broadcasted_iota(jnp.int32, (TOP_K, tm), 0)
    idx_out = jnp.zeros((TOP_K, tm), jnp.int32)
    w_out = jnp.zeros((TOP_K, tm), F32)
    rank_out = jnp.zeros((TOP_K, tm), F32)
    for k in range(TOP_K):
        rk = jnp.zeros((1, tm), F32)
        for g in range(N_GROUPS):
            rk = rk + jnp.sum(jnp.where((rowi + g * ge) == idx_rows[k], rank_full[g * ge:(g + 1) * ge], 0.0),
                              axis=0, keepdims=True)
        idx_out = jnp.where(rowk == k, idx_rows[k], idx_out)
        w_out = jnp.where(rowk == k, w_rows[k] / w_sum * ROUTED_SCALE, w_out)
        rank_out = jnp.where(rowk == k, rk, rank_out)
    idx_ref[...] = idx_out
    w_ref[...] = w_out
    rank_ref[...] = rank_out.astype(jnp.int32)


def _router_call(x, grp, layer, chunks, g, w_router, router_bias, ws_gate, ws_up, ws_down):
    t, d = x.shape
    e = w_router.shape[1]
    ds = ws_gate.shape[1]
    tm = grp.tm
    assert e // N_GROUPS == SUBLANE
    tri = jnp.triu(jnp.ones((tm, tm), BF16), 1)
    kern = functools.partial(_router_kernel, n_experts=e)
    row = pl.BlockSpec((TOP_K, tm), lambda i: (0, i))
    full = lambda a, b: pl.BlockSpec((a, b), lambda i: (0, 0))
    return pl.pallas_call(
        kern,
        grid=(t // tm,),
        in_specs=[pl.BlockSpec((tm, d), lambda i: (i, 0)),
                  grp.mod_spec(layer, chunks[0], d),
                  grp.mod_spec(layer, chunks[1], d),
                  full(1, d), full(e, d), full(e, 1), full(tm, tm), full(d, ds), full(d, ds), full(ds, d)],
        out_specs=[pl.BlockSpec((tm, d // LANE, LANE), lambda i: (i, 0, 0)),
                   pl.BlockSpec((tm, d), lambda i: (i, 0)), row, row, row,
                   pl.BlockSpec((None, e, LANE), lambda i: (i, 0, 0))],
        out_shape=[jax.ShapeDtypeStruct((t, d // LANE, LANE), F32), jax.ShapeDtypeStruct((t, d), F32),
                   jax.ShapeDtypeStruct((TOP_K, t), jnp.int32), jax.ShapeDtypeStruct((TOP_K, t), F32),
                   jax.ShapeDtypeStruct((TOP_K, t), jnp.int32), jax.ShapeDtypeStruct((t // tm, e, LANE), F32)],
        compiler_params=_cparams(("parallel",)),
        name="router",
    )(x, grp.mod_arr, grp.mod_arr, g.reshape(1, d), jnp.transpose(w_router), router_bias.reshape(e, 1).astype(F32),
      tri, ws_gate, ws_up, ws_down)


def _dispatch_kernel(dest_ref, h_ref, xs_hbm, sem, *, tc):
    def issue(r, carry):
        pltpu.make_async_copy(h_ref.at[lax.rem(r, tc)], xs_hbm.at[dest_ref[r]], sem).start()
        return carry

    lax.fori_loop(0, tc * TOP_K, issue, 0, unroll=8)
    for _ in range(TOP_K):
        pltpu.make_async_copy(h_ref, xs_hbm.at[pl.ds(0, tc)], sem).wait()


def _dispatch_call(dest_km, h3, tc):
    t, s, lanes = h3.shape
    kern = functools.partial(_dispatch_kernel, tc=tc)
    return pl.pallas_call(
        kern,
        grid=(t // tc,),
        in_specs=[pl.BlockSpec((None, None, TOP_K * tc), lambda i: (i, 0, 0), memory_space=pltpu.SMEM),
                  pl.BlockSpec((tc, s, lanes), lambda i: (i, 0, 0))],
        out_specs=pl.BlockSpec(memory_space=pl.ANY),
        out_shape=jax.ShapeDtypeStruct((t * TOP_K, s, lanes), F32),
        scratch_shapes=[pltpu.SemaphoreType.DMA(())],
        compiler_params=_cparams(("arbitrary",)),
        name="dispatch",
    )(dest_km, h3)


def _experts_kernel(iblk_ref, ie_ref, ilo_ref, ihi_ref, ilast_ref, x_ref, wg_ref, wu_ref, wd_ref, y_ref,
                    wg_scr, wu_scr, wd_scr, y_scr, *, blk):
    i = pl.program_id(0)
    lo = ilo_ref[i]
    hi = ihi_ref[i]
    prev = ie_ref[jnp.maximum(i - 1, 0)]
    d = y_scr.shape[1]

    @pl.when((i == 0) | (prev != ie_ref[i]))
    def _():
        wg_scr[...] = wg_ref[...].astype(BF16)
        wu_scr[...] = wu_ref[...].astype(BF16)
        wd_scr[...] = wd_ref[...].astype(BF16)

    @pl.when(hi > lo)
    def _():
        xb = x_ref[...].reshape(blk, d).astype(BF16)
        a = _silu(jnp.dot(xb, wg_scr[...], preferred_element_type=F32))
        a = a * jnp.dot(xb, wu_scr[...], preferred_element_type=F32)
        res = jnp.dot(a.astype(BF16), wd_scr[...], preferred_element_type=F32)
        row = lax.broadcasted_iota(jnp.int32, (blk, 1), 0)
        mine = (row >= lo) & (row < hi)

        @pl.when(lo == 0)
        def _():
            y_scr[...] = jnp.where(mine, res, 0.0)

        @pl.when(lo > 0)
        def _():
            y_scr[...] = jnp.where(mine, res, y_scr[...])

    @pl.when(ilast_ref[i] == 1)
    def _():
        y_ref[...] = y_scr[...].reshape(y_ref.shape)


def _experts_call(items, xs, w_gate, w_up, w_down, layer, blk):
    iblk, ie, ilo, ihi, ilast = items
    n_rows, s, lanes = xs.shape
    d = s * lanes
    de = w_gate.shape[3]
    kern = functools.partial(_experts_kernel, blk=blk)
    wspec = lambda a, b: pl.BlockSpec((None, None, a, b), lambda i, ib, ie_, *_: (layer, ie_[i], 0, 0))
    rows = pl.BlockSpec((blk, s, lanes), lambda i, ib, *_: (ib[i], 0, 0))
    grid_spec = pltpu.PrefetchScalarGridSpec(
        num_scalar_prefetch=5,
        grid=(iblk.shape[0],),
        in_specs=[rows, wspec(d, de), wspec(d, de), wspec(de, d)],
        out_specs=rows,
        scratch_shapes=[pltpu.VMEM((d, de), BF16), pltpu.VMEM((d, de), BF16), pltpu.VMEM((de, d), BF16),
                        pltpu.VMEM((blk, d), F32)],
    )
    return pl.pallas_call(
        kern,
        grid_spec=grid_spec,
        out_shape=jax.ShapeDtypeStruct((n_rows, s, lanes), F32),
        compiler_params=_cparams(("arbitrary",)),
        name="experts",
    )(iblk, ie, ilo, ihi, ilast, xs, w_gate, w_up, w_down)


def _combine_kernel(pos_ref, x_ref, sh_ref, g2_ref, fg_ref, w_ref, y_hbm, o_ref, buf, sem, *, tc, final):
    def issue(r, carry):
        pltpu.make_async_copy(y_hbm.at[pos_ref[r]], buf.at[r], sem).start()
        return carry

    lax.fori_loop(0, tc * TOP_K, issue, 0, unroll=8)
    pltpu.make_async_copy(y_hbm.at[pl.ds(0, tc * TOP_K)], buf, sem).wait()
    d = x_ref.shape[1]
    acc = buf[pl.ds(0, tc)].reshape(tc, d) * w_ref[:, 0:1]
    for k in range(1, TOP_K):
        acc = acc + buf[pl.ds(k * tc, tc)].reshape(tc, d) * w_ref[:, k:k + 1]
    out = x_ref[...] + g2_ref[...] * (acc + sh_ref[...])
    o_ref[...] = _rms(out) * fg_ref[...] if final else out


def _combine_call(pos_km, wts, x, shared, grp, layer, gate_chunk, final_g, y_sorted, tc, final):
    t, d = x.shape
    kern = functools.partial(_combine_kernel, tc=tc, final=final)
    seq = grp.seq
    if grp.per_token:
        g2 = pl.BlockSpec((None, tc, d), lambda i: (layer, i, gate_chunk))
    else:
        g2 = pl.BlockSpec((None, None, 1, d), lambda i: (layer, (i * tc) // seq, 0, gate_chunk))
    tile = pl.BlockSpec((tc, d), lambda i: (i, 0))
    return pl.pallas_call(
        kern,
        grid=(t // tc,),
        in_specs=[pl.BlockSpec((None, None, TOP_K * tc), lambda i: (i, 0, 0), memory_space=pltpu.SMEM),
                  tile, tile, g2,
                  pl.BlockSpec((1, d), lambda i: (0, 0)),
                  pl.BlockSpec((tc, TOP_K), lambda i: (i, 0)),
                  pl.BlockSpec(memory_space=pl.ANY)],
        out_specs=tile,
        out_shape=jax.ShapeDtypeStruct((t, d), F32),
        scratch_shapes=[pltpu.VMEM((TOP_K * tc,) + y_sorted.shape[1:], F32), pltpu.SemaphoreType.DMA(())],
        compiler_params=_cparams(("arbitrary",)),
        name="combine",
    )(pos_km, x, shared, grp.mod_arr, final_g.reshape(1, d), wts, y_sorted)


def _route_plan(idx_t, rank_t, counts, tm, blk, tc):
    _, t = idx_t.shape
    n_tiles, n_experts = counts.shape
    n_rows = t * TOP_K
    total = jnp.sum(counts, axis=0)
    e_end = jnp.cumsum(total)
    e_start = e_end - total
    table = e_start[None, :] + jnp.cumsum(counts, axis=0) - counts
    idx4 = idx_t.reshape(TOP_K, n_tiles, tm, 1)
    hit = idx4 == jnp.arange(n_experts, dtype=jnp.int32)
    dest = jnp.sum(jnp.where(hit, table[None, :, None, :], 0), axis=-1).reshape(TOP_K, t) + rank_t
    dest_km = dest.reshape(TOP_K, t // tc, tc).transpose(1, 0, 2).reshape(t // tc, 1, TOP_K * tc)

    n_blocks = n_rows // blk
    b_cut = jnp.arange(n_blocks, dtype=jnp.int32) * blk
    e_cut = e_start[1:].astype(jnp.int32)
    n_items = n_blocks + n_experts - 1
    b_at = (jnp.arange(n_blocks, dtype=jnp.int32)
            + jnp.sum(e_cut[None, :] < b_cut[:, None], axis=1).astype(jnp.int32))
    e_at = (jnp.arange(n_experts - 1, dtype=jnp.int32)
            + jnp.minimum(e_cut // blk + 1, n_blocks).astype(jnp.int32))
    cuts = jnp.zeros((n_items,), jnp.int32).at[b_at].set(b_cut).at[e_at].set(e_cut)
    ends = jnp.concatenate([cuts[1:], jnp.array([n_rows], jnp.int32)])
    iblk = jnp.minimum(cuts // blk, n_blocks - 1).astype(jnp.int32)
    ie = jnp.minimum(jnp.sum(e_end[None, :] <= cuts[:, None], axis=1), n_experts - 1).astype(jnp.int32)
    ilo = (cuts - iblk * blk).astype(jnp.int32)
    ihi = (ends - iblk * blk).astype(jnp.int32)
    ilast = jnp.concatenate([iblk[1:] != iblk[:-1], jnp.array([True])]).astype(jnp.int32)
    return dest_km, (iblk, ie, ilo, ihi, ilast)


def _bucket_bias(dist, rel_bias):
    n = jnp.maximum(dist, 0)
    exact = N_BUCKETS // 2
    nf = jnp.maximum(n, exact).astype(F32)
    large = exact + (jnp.log(nf / exact) / math.log(MAX_DISTANCE / exact) * (N_BUCKETS - exact)).astype(jnp.int32)
    bucket = jnp.where(n < exact, n, jnp.minimum(large, N_BUCKETS - 1))
    return rel_bias[bucket].astype(F32)


def _prompt_bias_table(rel_bias, tq, tk):
    ratio = tq // tk
    assert tk + 1 >= MAX_DISTANCE
    n_heads = rel_bias.shape[1]
    length = tq + tk
    tabs = []
    for b in range(ratio + 2):
        d0 = (b - (ratio - 1)) * tk
        dist = d0 - (tk - 1) + jnp.arange(length, dtype=jnp.int32)
        f = jnp.where((dist >= 0)[:, None], _bucket_bias(dist, rel_bias), -jnp.inf)
        g = jnp.transpose(f)
        flat = jnp.tile(g, (1, tq + 1))[:, :tq * (length + 1)]
        hankel = flat.reshape(n_heads, tq, length + 1)[:, :, :tk]
        tabs.append(hankel[:, :, ::-1])
    return jnp.stack(tabs, axis=1)


def _sample_bias_tables(rel_bias, steps, past_len, page):
    assert page + 1 >= MAX_DISTANCE
    qpos = past_len + jnp.arange(steps, dtype=jnp.int32)
    nrow = C_HEADS * 2 * steps
    tabs = []
    for first in (past_len - 2 * page, past_len - page):
        kpos = first + jnp.arange(page, dtype=jnp.int32)
        b = _bucket_bias(qpos[:, None] - kpos[None, :], rel_bias)
        b = jnp.transpose(b, (2, 0, 1))
        b = jnp.broadcast_to(b[:, None], (C_HEADS, 2, steps, page)).reshape(nrow, page)
        own = (jnp.arange(nrow)[:, None] // (2 * steps)) == jnp.arange(C_HEADS)[None, :]
        tabs.append(jnp.where(own[:, None, :], b[:, :, None], -jnp.inf).reshape(nrow, page * C_HEADS))
    bp = jnp.stack(tabs)
    dn = qpos[:, None] - qpos[None, :]
    bn = jnp.where((dn >= 0)[..., None], _bucket_bias(dn, rel_bias), -jnp.inf)
    bn = jnp.transpose(bn, (2, 0, 1))
    bn = jnp.broadcast_to(bn[:, None], (C_HEADS, 2, steps, steps)).reshape(C_HEADS * 2 * steps, steps)
    return bp, bn


def _trunk(x3, grp, p, past, depth):
    bsz, seq, d = x3.shape
    t = bsz * seq
    is_sample = past is not None
    x = x3.reshape(t, d)
    hw = B_HEADS * B_DK
    wdt = A_GROUPS * LANE
    tm = grp.tm
    blk = 256
    tc = min(128, t)
    new_k, new_v, new_s, new_conv, new_va = [], [], [], [], []
    for l in range(depth):
        proj = _inproj_call(x, grp, l, (1, 0), p['norm_mix_g'][l], p['w_in_main'][l], p['tn_in'])
        proj_ab = _inproj_call(x, grp, l, (1, 0), p['norm_mix_g'][l], p['w_in_ab'][l], LANE)
        proj3 = proj.reshape(bsz, seq, -1)

        c_a = min(seq, A_CHUNK)
        causal = jnp.tril(jnp.ones((c_a, c_a), bool))
        w_s = jnp.where(causal[None], p['spatial_w'][l][:, :c_a, :c_a], 0.0)
        sb = p['spatial_b'][l][:, :c_a]
        if c_a < A_CHUNK:
            rep = A_CHUNK // c_a
            w_s = jax.vmap(lambda m: jnp.kron(jnp.eye(rep, dtype=F32), m))(w_s)
            sb = jnp.tile(sb, (1, rep))
        sb_full = jnp.repeat(jnp.transpose(sb), LANE, axis=1)
        y_a, v_norm = _mixa_call(proj, w_s, sb_full, p['ln_v_g'][l], p['ln_v_b'][l])

        a_b = proj_ab[:, :B_HEADS]
        b_b = proj_ab[:, B_HEADS:2 * B_HEADS]
        beta = jax.nn.sigmoid(b_b).reshape(bsz, seq, B_HEADS)
        g_log = (-jnp.exp(p['A_log'][l].astype(F32))
                 * jax.nn.softplus(a_b + p['dt_bias'][l].astype(F32))).reshape(bsz, seq, B_HEADS)
        qkv_raw = proj3[:, :, 2 * wdt:2 * wdt + 3 * hw]
        if is_sample:
            tr = lambda a: jnp.transpose(a, (1, 0, 2))
            y_b_t, s_new = _gdn_sample_call(tr(qkv_raw), tr(past['state_conv'][l]),
                                            tr(proj3[:, :, 2 * wdt + 3 * hw:2 * wdt + 4 * hw]),
                                            tr(g_log), tr(beta), past['state_gdn'][l], p['conv_w'][l],
                                            p['gdn_norm_g'][l])
            y_b = tr(y_b_t).reshape(t, hw).astype(BF16)
            conv_state = jnp.concatenate([past['state_conv'][l], qkv_raw], axis=1)[:, seq:]
        else:
            c_g = math.gcd(seq, GDN_CHUNK)
            gc_row = jnp.cumsum(jnp.transpose(g_log.reshape(bsz, seq // c_g, c_g, B_HEADS), (0, 1, 3, 2)), axis=-1)
            gc_tok = jnp.transpose(gc_row, (0, 1, 3, 2)).reshape(bsz, seq, B_HEADS)
            y_b3, s_new = _gdn_prompt_call(proj3, 2 * wdt // hw, p['conv_w'][l], gc_tok, beta, gc_row,
                                           p['gdn_norm_g'][l])
            y_b = y_b3.reshape(t, hw)
            conv_state = jnp.concatenate([jnp.zeros((bsz, CONV_W - 1, 3 * hw), F32), qkv_raw], axis=1)[:, seq:]

        c0 = 2 * wdt + 4 * hw
        cw = C_HEADS * 2 * C_DQ
        k_c = proj3[:, :, c0 + cw:c0 + 2 * cw]
        v_c = proj3[:, :, c0 + 2 * cw:c0 + 3 * cw]
        lam_init = 0.8 - 0.6 * math.exp(-0.3 * l)
        lam = (jnp.exp(jnp.sum(p['lambda_q1'][l].astype(F32) * p['lambda_k1'][l].astype(F32)))
               - jnp.exp(jnp.sum(p['lambda_q2'][l].astype(F32) * p['lambda_k2'][l].astype(F32))) + lam_init)
        lam = lam.reshape(1).astype(F32)
        if is_sample:
            y_c = _attn_sample_call(past['page_table'], lam, proj3[:, :, c0:c0 + cw], k_c, v_c,
                                    p['bias_new'], p['bias_past'], p['subln_g'][l], past['cache_k'],
                                    past['cache_v'], l, 1.0 - lam_init, p['ppg'])
            y_c = y_c.reshape(t, cw).astype(BF16)
        else:
            nb = c0 // (2 * C_DQ)
            y_c = _attn_prompt_call(proj3, nb, nb + C_HEADS, nb + 2 * C_HEADS, p['bias_prompt'], lam,
                                    p['subln_g'][l], 1.0 - lam_init, p['tq']).reshape(t, cw)

        gate0 = c0 + 3 * cw
        tn_m = 512
        merged = _merge_call(y_a, y_b, y_c, proj, gate0 // tn_m, p['w_branch'][l], tm, tn_m)
        x = _outproj_call(merged, x, grp, l, 2, p['w_out'][l], 512)

        h3, shared, idx_t, w_t, rank_t, cnt = _router_call(
            x, grp, l, (4, 3), p['norm_moe_g'][l], p['w_router'][l], p['router_bias'][l],
            p['ws_gate'][l], p['ws_up'][l], p['ws_down'][l])
        dest_km, items = _route_plan(idx_t, rank_t, cnt[:, :, 0].astype(jnp.int32), tm, blk, tc)
        xs = _dispatch_call(dest_km, h3, tc)
        y_sorted = _experts_call(items, xs, p['w_gate'], p['w_up'], p['w_down'], l, blk)
        x = _combine_call(dest_km, jnp.transpose(w_t), x, shared, grp, l, 5, p['final_norm_g'], y_sorted, tc,
                          l == depth - 1)

        k_heads, v_heads = _heads_call(proj, (c0 + cw) // cw, (c0 + 2 * cw) // cw, C_HEADS)
        new_k.append(k_heads.reshape(bsz, seq, C_HEADS, 2 * C_DQ))
        new_v.append(v_heads.reshape(bsz, seq, C_HEADS, 2 * C_DQ))
        new_s.append(s_new)
        new_conv.append(conv_state)
        if is_sample:
            new_va.append(v_norm.reshape(bsz, seq, wdt))
    chunk_v = jnp.stack(new_va) if is_sample else None
    return (x.reshape(bsz, seq, d), jnp.stack(new_k), jnp.stack(new_v), jnp.stack(new_s), jnp.stack(new_conv),
            chunk_v)


def kernel(x_prompt, x_sample, cache_k, cache_v, state_gdn, state_conv, page_table, c_prompt, c_sample,
           w_mod, b_mod, norm_mix_g, w_in, spatial_w, spatial_b, ln_v_g, ln_v_b, conv_w, A_log, dt_bias,
           gdn_norm_g, lambda_q1, lambda_k1, lambda_q2, lambda_k2, subln_g, rel_bias, w_branch, w_out,
           norm_moe_g, w_router, router_bias, w_gate, w_up, w_down, ws_gate, ws_up, ws_down, final_norm_g):
    depth = w_mod.shape[0]
    bsz, seq, d = x_prompt.shape
    dbsz, dseq, _ = x_sample.shape
    wdt = A_GROUPS * LANE
    hw = B_HEADS * B_DK
    n_ab = 2 * B_HEADS
    ab0 = 2 * wdt + 4 * hw
    w_in_main = jnp.concatenate([w_in[:, :, :ab0], w_in[:, :, ab0 + n_ab:]], axis=2)
    w_in_ab = jnp.pad(w_in[:, :, ab0:ab0 + n_ab], ((0, 0), (0, 0), (0, LANE - n_ab)))
    n_main = w_in_main.shape[2]
    tn_in = next(c for c in (1536, 1280, 1024, 512, 256, 128) if n_main % c == 0)

    mod = _mod_call(jnp.concatenate([c_prompt, c_sample], axis=0), w_mod, b_mod)
    mod_prompt = mod[:, :bsz].reshape(depth, bsz, 1, mod.shape[2])
    mod_sample = jnp.repeat(mod[:, bsz:], dseq, axis=1)

    past_len = page_table.shape[1] * cache_k.shape[2]
    tq = min(512, seq)
    bias_past, bias_new = _sample_bias_tables(rel_bias, dseq, past_len, cache_k.shape[2])
    p = dict(norm_mix_g=norm_mix_g, w_in_main=w_in_main, w_in_ab=w_in_ab, tn_in=tn_in, spatial_w=spatial_w,
             spatial_b=spatial_b, ln_v_g=ln_v_g, ln_v_b=ln_v_b, conv_w=conv_w, A_log=A_log, dt_bias=dt_bias,
             gdn_norm_g=gdn_norm_g, lambda_q1=lambda_q1, lambda_k1=lambda_k1, lambda_q2=lambda_q2,
             lambda_k2=lambda_k2, subln_g=subln_g, w_branch=w_branch, w_out=w_out, norm_moe_g=norm_moe_g,
             w_router=w_router, router_bias=router_bias, w_gate=w_gate, w_up=w_up, w_down=w_down,
             ws_gate=ws_gate, ws_up=ws_up, ws_down=ws_down, final_norm_g=final_norm_g,
             bias_prompt=_prompt_bias_table(rel_bias, tq, min(LANE, seq)), tq=tq, bias_past=bias_past, bias_new=bias_new,
             ppg=math.gcd(page_table.shape[1], 4))
    past = dict(cache_k=cache_k, cache_v=cache_v,
                state_gdn=state_gdn, state_conv=state_conv, page_table=page_table)

    grp_p = _Group(mod_prompt, False, seq, min(512, seq))
    grp_s = _Group(mod_sample, True, dseq, min(512, dbsz * dseq))
    y_p, k_p, v_p, s_p, conv_p, _ = _trunk(x_prompt, grp_p, p, None, depth)
    y_s, k_s, v_s, s_s, conv_s, chunk_v = _trunk(x_sample, grp_s, p, past, depth)
    return (y_p, y_s, k_p, v_p, k_s, v_s, s_p, s_s, conv_p, conv_s, chunk_v)
```

```python
import functools
import math

import numpy as np
import jax
import jax.numpy as jnp
from jax import lax
from jax.experimental import pallas as pl
from jax.experimental.pallas import tpu as pltpu

F32 = jnp.float32
BF16 = jnp.bfloat16
EPS = 1e-6

LANE = 128
SUBLANE = 8
VMEM_LIMIT_BYTES = 56 * 1024 * 1024

A_GROUPS = 8
A_CHUNK = 128
B_HEADS = 8
B_DK = 128
CONV_W = 4
GDN_CHUNK = 64
C_HEADS = 8
C_DQ = 64
N_BUCKETS = 32
MAX_DISTANCE = 128
TOP_K = 8
N_GROUPS = 8
TOPK_GROUPS = 4
ROUTED_SCALE = 2.5
N_BRANCH = 3
GELU_C = float(np.float32(np.sqrt(2.0 / np.pi)))


def _cparams(sem):
    return pltpu.CompilerParams(dimension_semantics=sem, vmem_limit_bytes=VMEM_LIMIT_BYTES)


def _sigmoid(x):
    return 1.0 / (1.0 + jnp.exp(-x))


def _silu(x):
    return x * _sigmoid(x)


def _gelu(x):
    return x * (0.5 * (1.0 + jnp.tanh(GELU_C * (x + 0.044715 * (x * x * x)))))


def _dot(a, b):
    return jnp.dot(a.astype(BF16), b.astype(BF16), preferred_element_type=F32)


def _dot_nt(a, b):
    return lax.dot_general(a.astype(BF16), b.astype(BF16), (((1,), (1,)), ((), ())),
                           preferred_element_type=F32)


def _dot_tn(a, b):
    return lax.dot_general(a.astype(BF16), b.astype(BF16), (((0,), (0,)), ((), ())),
                           preferred_element_type=F32)


def _split2(a):
    hi = a.astype(BF16)
    return hi, (a - hi.astype(F32)).astype(BF16)


def _dot_hi(a, b):
    a0, a1 = _split2(a)
    b0, b1 = _split2(b)
    d = lambda x, y: jnp.dot(x, y, preferred_element_type=F32)
    return (d(a1, b0) + d(a0, b1)) + d(a0, b0)


def _rms(x):
    return x * lax.rsqrt(jnp.mean(x * x, axis=-1, keepdims=True) + EPS)


def _mod_kernel(c_ref, w_ref, b_ref, o_ref):
    o_ref[...] = _dot(_silu(c_ref[...]), w_ref[...]) + b_ref[...]


def _mod_call(c_all, w_mod, b_mod):
    depth, d, n = w_mod.shape
    m = c_all.shape[0]
    tn = 1536 if n % 1536 == 0 else n
    return pl.pallas_call(
        _mod_kernel,
        grid=(depth, n // tn),
        in_specs=[pl.BlockSpec((m, d), lambda l, j: (0, 0)),
                  pl.BlockSpec((None, d, tn), lambda l, j: (l, 0, j)),
                  pl.BlockSpec((None, 1, tn), lambda l, j: (l, 0, j))],
        out_specs=pl.BlockSpec((None, m, tn), lambda l, j: (l, 0, j)),
        out_shape=jax.ShapeDtypeStruct((depth, m, n), F32),
        compiler_params=_cparams(("parallel", "parallel")),
        name="mod",
    )(c_all, w_mod, b_mod.reshape(depth, 1, n))


class _Group:
    def __init__(self, mod_arr, per_token, seq, tm):
        self.mod_arr = mod_arr
        self.per_token = per_token
        self.seq = seq
        self.tm = tm

    def mod_spec(self, layer, chunk, d, tm=None):
        tm, seq = tm or self.tm, self.seq
        if self.per_token:
            return pl.BlockSpec((None, tm, d), lambda i, *_: (layer, i, chunk))
        return pl.BlockSpec((None, None, 1, d), lambda i, *_: (layer, (i * tm) // seq, 0, chunk))


def _inproj_kernel(x_ref, sc_ref, sh_ref, g_ref, w_ref, o_ref, h_scr):
    @pl.when(pl.program_id(1) == 0)
    def _():
        h = _rms(x_ref[...]) * g_ref[...]
        h_scr[...] = (h * (1.0 + sc_ref[...]) + sh_ref[...]).astype(BF16)

    o_ref[...] = jnp.dot(h_scr[...], w_ref[...].astype(BF16), preferred_element_type=F32)


def _inproj_call(x, grp, layer, chunks, g, w, tn):
    t, d = x.shape
    n = w.shape[1]
    tm = grp.tm
    if not grp.per_token and grp.seq % (2 * tm) == 0 and n % 1024 == 0 and tn > 1024:
        tm, tn = 2 * tm, 1024
    return pl.pallas_call(
        _inproj_kernel,
        grid=(t // tm, n // tn),
        in_specs=[pl.BlockSpec((tm, d), lambda i, j: (i, 0)),
                  grp.mod_spec(layer, chunks[0], d, tm),
                  grp.mod_spec(layer, chunks[1], d, tm),
                  pl.BlockSpec((1, d), lambda i, j: (0, 0)),
                  pl.BlockSpec((d, tn), lambda i, j: (0, j))],
        out_specs=pl.BlockSpec((tm, tn), lambda i, j: (i, j)),
        out_shape=jax.ShapeDtypeStruct((t, n), F32),
        scratch_shapes=[pltpu.VMEM((tm, d), BF16)],
        compiler_params=_cparams(("parallel", "arbitrary")),
        name="inproj",
    )(x, grp.mod_arr, grp.mod_arr, g.reshape(1, d), w)


def _mixa_kernel(u_ref, v_ref, w_ref, sb_ref, lg_ref, lb_ref, y_ref, vn_ref):
    v = _gelu(v_ref[...])
    xc = v - jnp.mean(v, axis=-1, keepdims=True)
    vn = xc * lax.rsqrt(jnp.mean(xc * xc, axis=-1, keepdims=True) + EPS) * lg_ref[...] + lb_ref[...]
    vn_ref[...] = vn
    vb = vn.astype(BF16)
    gd = LANE
    for g in range(A_GROUPS):
        sl = slice(g * gd, (g + 1) * gd)
        s = jnp.dot(w_ref[g].astype(BF16), vb[:, sl], preferred_element_type=F32) + sb_ref[:, sl]
        y_ref[:, sl] = (_gelu(u_ref[:, sl]) * s).astype(y_ref.dtype)


def _mixa_call(proj, w_chunk, sb_full, ln_g, ln_b):
    t = proj.shape[0]
    wdt = A_GROUPS * LANE
    c = A_CHUNK
    return pl.pallas_call(
        _mixa_kernel,
        grid=(t // c,),
        in_specs=[pl.BlockSpec((c, wdt), lambda i: (i, 0)),
                  pl.BlockSpec((c, wdt), lambda i: (i, 1)),
                  pl.BlockSpec((A_GROUPS, c, c), lambda i: (0, 0, 0)),
                  pl.BlockSpec((c, wdt), lambda i: (0, 0)),
                  pl.BlockSpec((1, wdt), lambda i: (0, 0)),
                  pl.BlockSpec((1, wdt), lambda i: (0, 0))],
        out_specs=[pl.BlockSpec((c, wdt), lambda i: (i, 0)),
                   pl.BlockSpec((c, wdt), lambda i: (i, 0))],
        out_shape=[jax.ShapeDtypeStruct((t, wdt), BF16),
                   jax.ShapeDtypeStruct((t, wdt), F32)],
        compiler_params=_cparams(("parallel",)),
        name="mixer_a",
    )(proj, proj, w_chunk, sb_full, ln_g.reshape(1, wdt), ln_b.reshape(1, wdt))


def _gdn_prompt_kernel(q_ref, k_ref, v_ref, z_ref, cw_ref, gct_ref, bt_ref, gcr_ref, ng_ref,
                       y_ref, s_ref, xp_scr, s_scr, *, chunk, n_chunks):
    c = chunk
    hw = B_HEADS * B_DK
    step = pl.program_id(1)

    @pl.when(step == 0)
    def _():
        s_scr[...] = jnp.zeros_like(s_scr)
        xp_scr[:, 0:SUBLANE, :] = jnp.zeros((3, SUBLANE, hw), F32)

    act = []
    for idx, r in enumerate((q_ref, k_ref, v_ref)):
        xp_scr[idx, SUBLANE:SUBLANE + c, :] = r[...]
        a = None
        for i in range(CONV_W):
            off = SUBLANE - (CONV_W - 1) + i
            term = xp_scr[idx, off:off + c, :] * cw_ref[i:i + 1, idx * hw:(idx + 1) * hw]
            a = term if a is None else a + term
        xp_scr[idx, 0:SUBLANE, :] = xp_scr[idx, c:c + SUBLANE, :]
        act.append(_silu(a))
    qa, ka, va = act

    gct = gct_ref[...]
    bt = bt_ref[...]
    gcr_all = gcr_ref[...]
    row = lax.broadcasted_iota(jnp.int32, (c, c), 0)
    col = lax.broadcasted_iota(jnp.int32, (c, c), 1)
    causal = row >= col
    strict = row > col
    levels = int(math.log2(c))
    ng = ng_ref[...]

    hs = range(B_HEADS)
    heads = lambda a: jnp.stack([a[:, h * B_DK:(h + 1) * B_DK] for h in hs], axis=0)
    bdot = lambda a, b: jnp.einsum('hij,hjk->hik', a.astype(BF16), b.astype(BF16), preferred_element_type=F32)

    def bdot_hi(a, b):
        a0, a1 = _split2(a)
        b0, b1 = _split2(b)
        d = lambda x, y: jnp.einsum('hij,hjk->hik', x, y, preferred_element_type=F32)
        return (d(a1, b0) + d(a0, b1)) + d(a0, b0)

    q = heads(qa)
    q = q * lax.rsqrt(jnp.sum(q * q, axis=-1, keepdims=True) + EPS) * (B_DK ** -0.5)
    k = heads(ka)
    k = k * lax.rsqrt(jnp.sum(k * k, axis=-1, keepdims=True) + EPS)
    v = heads(va)
    gcc = jnp.stack([gct[:, h:h + 1] for h in hs], axis=0)
    bc = jnp.stack([bt[:, h:h + 1] for h in hs], axis=0)
    gcr = jnp.stack([gcr_all[h:h + 1, :] for h in hs], axis=0)
    decay = jnp.exp(jnp.where(causal[None], gcc - gcr, -jnp.inf))
    qkk = jnp.einsum('hid,hjd->hij', jnp.concatenate([q, k], axis=1).astype(BF16), k.astype(BF16),
                     preferred_element_type=F32)
    qk = qkk[:, :c] * decay
    x = -jnp.where(strict[None], qkk[:, c:] * decay * bc, 0.0)
    r_acc = x
    p = x
    for _ in range(levels - 1):
        p = bdot_hi(p, p)
        r_acc = r_acc + p + bdot_hi(r_acc, p)
    egc = jnp.exp(gcc)
    rhs = jnp.concatenate([v * bc, k * (bc * egc)], axis=2)
    uw = rhs + bdot_hi(r_acc, rhs)
    u = uw[:, :, :B_DK]
    w = uw[:, :, B_DK:]
    s_old = s_scr[...]
    wq = bdot(jnp.concatenate([w, q * egc], axis=1), s_old)
    v_new = u - wq[:, :c]
    o = wq[:, c:] + bdot(qk, v_new)
    g_last = gcr[:, :, c - 1:c]
    kd = k * jnp.exp(g_last - gcc)
    upd = jnp.stack([_dot_tn(kd[h], v_new[h]) for h in hs], axis=0)
    s_scr[...] = s_old * jnp.exp(g_last) + upd
    on = _rms(o) * ng
    for h in hs:
        sl = slice(h * B_DK, (h + 1) * B_DK)
        y_ref[:, sl] = (on[h] * _silu(z_ref[:, sl])).astype(y_ref.dtype)

    @pl.when(step == n_chunks - 1)
    def _():
        s_ref[...] = s_scr[...]


def _gdn_prompt_call(proj3, col0, conv_w, gc_tok, beta_tok, gc_row, norm_g):
    bsz, seq, _ = proj3.shape
    c = math.gcd(seq, GDN_CHUNK)
    n = seq // c
    hw = B_HEADS * B_DK
    kern = functools.partial(_gdn_prompt_kernel, chunk=c, n_chunks=n)
    tok = lambda j: pl.BlockSpec((None, c, hw), lambda b, s: (b, s, col0 + j))
    return pl.pallas_call(
        kern,
        grid=(bsz, n),
        in_specs=[tok(0), tok(1), tok(2), tok(3),
                  pl.BlockSpec((CONV_W, 3 * hw), lambda b, s: (0, 0)),
                  pl.BlockSpec((None, c, B_HEADS), lambda b, s: (b, s, 0)),
                  pl.BlockSpec((None, c, B_HEADS), lambda b, s: (b, s, 0)),
                  pl.BlockSpec((None, None, B_HEADS, c), lambda b, s: (b, s, 0, 0)),
                  pl.BlockSpec((1, B_DK), lambda b, s: (0, 0))],
        out_specs=[pl.BlockSpec((None, c, hw), lambda b, s: (b, s, 0)),
                   pl.BlockSpec((None, B_HEADS, B_DK, B_DK), lambda b, s: (b, 0, 0, 0))],
        out_shape=[jax.ShapeDtypeStruct((bsz, seq, hw), BF16),
                   jax.ShapeDtypeStruct((bsz, B_HEADS, B_DK, B_DK), F32)],
        scratch_shapes=[pltpu.VMEM((3, c + SUBLANE, hw), F32),
                        pltpu.VMEM((B_HEADS, B_DK, B_DK), F32)],
        compiler_params=_cparams(("parallel", "arbitrary")),
        name="gdn_prompt",
    )(proj3, proj3, proj3, proj3, conv_w, gc_tok, beta_tok, gc_row, norm_g.reshape(1, B_DK))


def _gdn_sample_kernel(x_ref, p_ref, z_ref, g_ref, b_ref, s0_ref, cw_ref, ng_ref,
                       y_ref, s_ref, act_scr, o_scr, *, steps, bb):
    hw = B_HEADS * B_DK
    xp = [p_ref[i] for i in range(CONV_W - 1)] + [x_ref[t] for t in range(steps)]
    for t in range(steps):
        a = None
        for i in range(CONV_W):
            term = xp[t + i] * cw_ref[i:i + 1, :]
            a = term if a is None else a + term
        act_scr[t] = _silu(a)

    def head(h, carry):
        off = pl.multiple_of(h * B_DK, B_DK)
        lane_h = lax.broadcasted_iota(jnp.int32, (bb, B_HEADS), 1) == h
        q_t, k_t, v_r, a_c, b_c = [], [], [], [], []
        for t in range(steps):
            q = act_scr[t, :, pl.ds(off, B_DK)]
            q = q * lax.rsqrt(jnp.sum(q * q, axis=-1, keepdims=True) + EPS) * (B_DK ** -0.5)
            k = act_scr[t, :, pl.ds(hw + off, B_DK)]
            k = k * lax.rsqrt(jnp.sum(k * k, axis=-1, keepdims=True) + EPS)
            q_t.append(q.T)
            k_t.append(k.T)
            v_r.append(act_scr[t, :, pl.ds(2 * hw + off, B_DK)])
            a_c.append(jnp.exp(jnp.sum(jnp.where(lane_h, g_ref[t], 0.0), axis=-1, keepdims=True)))
            b_c.append(jnp.sum(jnp.where(lane_h, b_ref[t], 0.0), axis=-1, keepdims=True))
        for b in range(bb):
            s = s0_ref[b, h]
            for t in range(steps):
                a = a_c[t][b:b + 1, :]
                be = b_c[t][b:b + 1, :]
                kc = k_t[t][:, b:b + 1]
                r = jnp.sum(kc * s, axis=0, keepdims=True)
                d = be * (v_r[t][b:b + 1, :] - a * r)
                s = a * s + kc * d
                o_scr[t, b:b + 1, pl.ds(off, B_DK)] = jnp.sum(q_t[t][:, b:b + 1] * s, axis=0, keepdims=True)
            s_ref[b, h] = s
        return carry

    lax.fori_loop(0, B_HEADS, head, 0)

    ng = ng_ref[...]
    for t in range(steps):
        for h in range(B_HEADS):
            sl = slice(h * B_DK, (h + 1) * B_DK)
            y_ref[t, :, sl] = _rms(o_scr[t, :, sl]) * ng * _silu(z_ref[t, :, sl])


def _gdn_sample_call(x_t, prev_t, z_t, g_t, beta_t, s0, conv_w, norm_g):
    steps, bsz, _ = x_t.shape
    hw = B_HEADS * B_DK
    bb = SUBLANE
    kern = functools.partial(_gdn_sample_kernel, steps=steps, bb=bb)
    return pl.pallas_call(
        kern,
        grid=(bsz // bb,),
        in_specs=[pl.BlockSpec((steps, bb, 3 * hw), lambda i: (0, i, 0)),
                  pl.BlockSpec((CONV_W - 1, bb, 3 * hw), lambda i: (0, i, 0)),
                  pl.BlockSpec((steps, bb, hw), lambda i: (0, i, 0)),
                  pl.BlockSpec((steps, bb, B_HEADS), lambda i: (0, i, 0)),
                  pl.BlockSpec((steps, bb, B_HEADS), lambda i: (0, i, 0)),
                  pl.BlockSpec((bb, B_HEADS, B_DK, B_DK), lambda i: (i, 0, 0, 0)),
                  pl.BlockSpec((CONV_W, 3 * hw), lambda i: (0, 0)),
                  pl.BlockSpec((1, B_DK), lambda i: (0, 0))],
        out_specs=[pl.BlockSpec((steps, bb, hw), lambda i: (0, i, 0)),
                   pl.BlockSpec((bb, B_HEADS, B_DK, B_DK), lambda i: (i, 0, 0, 0))],
        out_shape=[jax.ShapeDtypeStruct((steps, bsz, hw), F32),
                   jax.ShapeDtypeStruct((bsz, B_HEADS, B_DK, B_DK), F32)],
        scratch_shapes=[pltpu.VMEM((steps, bb, 3 * hw), F32),
                        pltpu.VMEM((steps, bb, hw), F32)],
        compiler_params=_cparams(("parallel",)),
        name="gdn_sample",
    )(x_t, prev_t, z_t, g_t, beta_t, s0, conv_w, norm_g.reshape(1, B_DK))


def _attn_prompt_kernel(lam_ref, q_ref, k_ref, v_ref, bias_ref, sg_ref, y_ref, m_scr, l_scr, acc_scr,
                        *, tq, tk, out_scale):
    qi = pl.program_id(2)
    dv = 2 * C_DQ
    q = q_ref[...] * (C_DQ ** -0.5)
    lane = lax.broadcasted_iota(jnp.int32, (tq, dv), 1)
    q2 = jnp.concatenate([jnp.where(lane < C_DQ, q, 0.0), jnp.where(lane >= C_DQ, q, 0.0)], axis=0).astype(BF16)
    ratio = tq // tk

    def scores(j):
        start = pl.multiple_of(j * tk, tk)
        bias = bias_ref[jnp.minimum(qi * ratio - j + (ratio - 1), ratio + 1)]
        return _dot_nt(q2, k_ref[pl.ds(start, tk), :]) + jnp.concatenate([bias, bias], axis=0)

    m_scr[...] = jnp.full_like(m_scr, -jnp.inf)

    def sweep_max(jj, carry):
        m = m_scr[...]
        for u in range(ratio):
            m = jnp.maximum(m, scores(jj * ratio + u))
        m_scr[...] = m
        return carry

    lax.fori_loop(0, qi + 1, sweep_max, 0)
    m_scr[...] = jnp.broadcast_to(jnp.max(m_scr[...], axis=-1, keepdims=True), m_scr.shape)
    l_scr[...] = jnp.zeros_like(l_scr)

    def sweep_sum(jj, carry):
        m = m_scr[...]
        l = l_scr[...]
        for u in range(ratio):
            l = l + jnp.exp(scores(jj * ratio + u) - m)
        l_scr[...] = l
        return carry

    lax.fori_loop(0, qi + 1, sweep_sum, 0)
    l_scr[...] = jnp.broadcast_to(1.0 / jnp.sum(l_scr[...], axis=-1, keepdims=True), l_scr.shape)
    acc_scr[...] = jnp.zeros_like(acc_scr)
    lam = lam_ref[0]

    def sweep_values(jj, carry):
        acc = acc_scr[...]
        for u in range(ratio):
            j = jj * ratio + u
            p = jnp.exp(scores(j) - m_scr[...]) * l_scr[...]
            start = pl.multiple_of(j * tk, tk)
            acc = acc + _dot(p[:tq] - lam * p[tq:], v_ref[pl.ds(start, tk), :])
        acc_scr[...] = acc
        return carry

    lax.fori_loop(0, qi + 1, sweep_values, 0)
    y_ref[...] = (_rms(acc_scr[...]) * sg_ref[...] * out_scale).astype(y_ref.dtype)


def _attn_prompt_call(proj3, qcol, kcol, vcol, bias_tab, lam, subln_g, out_scale, tq):
    bsz, seq, _ = proj3.shape
    dv = 2 * C_DQ
    tk = bias_tab.shape[3]
    kern = functools.partial(_attn_prompt_kernel, tq=tq, tk=tk, out_scale=out_scale)
    return pl.pallas_call(
        kern,
        grid=(bsz, C_HEADS, seq // tq),
        in_specs=[pl.BlockSpec(memory_space=pltpu.SMEM),
                  pl.BlockSpec((None, tq, dv), lambda b, h, i: (b, i, qcol + h)),
                  pl.BlockSpec((None, seq, dv), lambda b, h, i: (b, 0, kcol + h)),
                  pl.BlockSpec((None, seq, dv), lambda b, h, i: (b, 0, vcol + h)),
                  pl.BlockSpec((None,) + bias_tab.shape[1:], lambda b, h, i: (h, 0, 0, 0)),
                  pl.BlockSpec((1, dv), lambda b, h, i: (0, 0))],
        out_specs=pl.BlockSpec((None, tq, dv), lambda b, h, i: (b, i, h)),
        out_shape=jax.ShapeDtypeStruct((bsz, seq, C_HEADS * dv), BF16),
        scratch_shapes=[pltpu.VMEM((2 * tq, tk), F32), pltpu.VMEM((2 * tq, tk), F32),
                        pltpu.VMEM((tq, dv), F32)],
        compiler_params=_cparams(("parallel", "parallel", "arbitrary")),
        name="attn_prompt",
    )(lam, proj3, proj3, proj3, bias_tab, subln_g.reshape(1, dv))


def _attn_sample_kernel(pt_ref, lam_ref, q_ref, kn_ref, vn_ref, bnew_ref, bias_ref, sg_ref, *rest,
                        steps, ppg, n_groups, out_scale):
    k_refs = rest[:ppg]
    v_refs = rest[ppg:2 * ppg]
    y_ref, m_scr, l_scr, acc_scr = rest[2 * ppg:]
    g = pl.program_id(1)
    dv = 2 * C_DQ
    nrow = C_HEADS * 2 * steps
    rr = lax.broadcasted_iota(jnp.int32, (nrow, dv), 0)
    cc = lax.broadcasted_iota(jnp.int32, (nrow, dv), 1)
    q = jnp.where(cc // C_DQ == (rr // steps) % 2, q_ref[...] * (C_DQ ** -0.5), 0.0)

    @pl.when(g == 0)
    def _():
        s_cols = [jnp.sum(q * kn_ref[j], axis=-1, keepdims=True) + bnew_ref[:, j:j + 1] for j in range(steps)]
        m = functools.reduce(jnp.maximum, s_cols)
        p_cols = [jnp.exp(sc - m) for sc in s_cols]
        m_scr[...] = m
        l_scr[...] = functools.reduce(lambda a, b: a + b, p_cols)
        acc = p_cols[0] * vn_ref[0]
        for j in range(1, steps):
            acc = acc + p_cols[j] * vn_ref[j]
        acc_scr[...] = acc

    qb = q.astype(BF16)
    s_pages = []
    for pidx in range(ppg):
        is_last = jnp.logical_and(g == n_groups - 1, pidx == ppg - 1).astype(jnp.int32)
        s_pages.append(_dot_nt(qb, k_refs[pidx][...]) + bias_ref[is_last])
    m_old = m_scr[...]
    m_new = functools.reduce(jnp.maximum, [m_old] + [jnp.max(s, axis=-1, keepdims=True) for s in s_pages])
    alpha = jnp.exp(m_old - m_new)
    l_new = alpha * l_scr[...]
    acc = alpha * acc_scr[...]
    for pidx in range(ppg):
        p = jnp.exp(s_pages[pidx] - m_new)
        l_new = l_new + jnp.sum(p, axis=-1, keepdims=True)
        acc = acc + _dot(p, v_refs[pidx][...])
    l_scr[...] = l_new
    acc_scr[...] = acc
    m_scr[...] = m_new

    @pl.when(g == n_groups - 1)
    def _():
        o = acc_scr[...] / l_scr[...]
        lam = lam_ref[0]
        for h in range(C_HEADS):
            r0 = h * 2 * steps
            od = o[r0:r0 + steps] - lam * o[r0 + steps:r0 + 2 * steps]
            y_ref[:, h * dv:(h + 1) * dv] = _rms(od) * sg_ref[...] * out_scale


def _attn_sample_call(page_table, lam, q, k_new, v_new, bias_new, bias_past, subln_g, cache_k, cache_v,
                      layer, out_scale, ppg):
    bsz, steps, width = q.shape
    n_pages = page_table.shape[1]
    depth, n_pool, page, n_heads, dv = cache_k.shape
    n_groups = n_pages // ppg
    nrow = n_heads * 2 * steps
    kern = functools.partial(_attn_sample_kernel, steps=steps, ppg=ppg, n_groups=n_groups, out_scale=out_scale)

    def rows(a):
        a = a.reshape(bsz, steps, n_heads, 1, dv)
        return jnp.broadcast_to(a, (bsz, steps, n_heads, 2 * steps, dv)).reshape(bsz, steps, nrow, dv)

    q_rows = jnp.transpose(q.reshape(bsz, steps, n_heads, dv), (0, 2, 1, 3))
    q_rows = jnp.broadcast_to(q_rows[:, :, None], (bsz, n_heads, 2, steps, dv)).reshape(bsz, nrow, dv)
    pages_k = cache_k.reshape(depth, n_pool, page * n_heads, dv)
    pages_v = cache_v.reshape(depth, n_pool, page * n_heads, dv)

    def page_spec(pidx):
        return pl.BlockSpec((None, None, page * n_heads, dv),
                            lambda b, g, pt: (layer, pt[b * n_pages + g * ppg + pidx], 0, 0))

    new_spec = pl.BlockSpec((None, steps, nrow, dv), lambda b, g, pt: (b, 0, 0, 0))
    grid_spec = pltpu.PrefetchScalarGridSpec(
        num_scalar_prefetch=1,
        grid=(bsz, n_groups),
        in_specs=[pl.BlockSpec(memory_space=pltpu.SMEM),
                  pl.BlockSpec((None, nrow, dv), lambda b, g, pt: (b, 0, 0)), new_spec, new_spec,
                  pl.BlockSpec((nrow, steps), lambda b, g, pt: (0, 0)),
                  pl.BlockSpec((2, nrow, page * n_heads), lambda b, g, pt: (0, 0, 0)),
                  pl.BlockSpec((1, dv), lambda b, g, pt: (0, 0))]
                 + [page_spec(p) for p in range(ppg)] + [page_spec(p) for p in range(ppg)],
        out_specs=pl.BlockSpec((None, steps, width), lambda b, g, pt: (b, 0, 0)),
        scratch_shapes=[pltpu.VMEM((nrow, 1), F32), pltpu.VMEM((nrow, 1), F32), pltpu.VMEM((nrow, dv), F32)],
    )
    return pl.pallas_call(
        kern,
        grid_spec=grid_spec,
        out_shape=jax.ShapeDtypeStruct((bsz, steps, width), F32),
        compiler_params=_cparams(("parallel", "arbitrary")),
        name="attn_sample",
    )(page_table.reshape(-1), lam, q_rows, rows(k_new), rows(v_new), bias_new, bias_past, subln_g.reshape(1, dv),
      *([pages_k] * ppg), *([pages_v] * ppg))


def _heads_kernel(k_ref, v_ref, ko_ref, vo_ref):
    ko_ref[...] = k_ref[...].reshape(ko_ref.shape)
    vo_ref[...] = v_ref[...].reshape(vo_ref.shape)


def _heads_call(proj, kblk, vblk, n_heads):
    t = proj.shape[0]
    dv = 2 * C_DQ
    tm = min(256, t)
    out = jax.ShapeDtypeStruct((t, n_heads, dv), proj.dtype)
    ospec = pl.BlockSpec((tm, n_heads, dv), lambda i: (i, 0, 0))
    return pl.pallas_call(
        _heads_kernel,
        grid=(t // tm,),
        in_specs=[pl.BlockSpec((tm, n_heads * dv), lambda i: (i, kblk)),
                  pl.BlockSpec((tm, n_heads * dv), lambda i: (i, vblk))],
        out_specs=[ospec, ospec],
        out_shape=[out, out],
        compiler_params=_cparams(("parallel",)),
        name="kv_heads",
    )(proj, proj)


def _merge_kernel(ya_ref, yb_ref, yc_ref, ga_ref, gb_ref, gc_ref, w_ref, o_ref):
    acc = None
    for n, (y_ref, g_ref) in enumerate(((ya_ref, ga_ref), (yb_ref, gb_ref), (yc_ref, gc_ref))):
        term = _sigmoid(g_ref[...]) * jnp.dot(y_ref[...].astype(BF16), w_ref[n].astype(BF16),
                                              preferred_element_type=F32)
        acc = term if acc is None else acc + term
    o_ref[...] = acc.astype(o_ref.dtype)


def _merge_call(ya, yb, yc, proj, gate_col0, w_branch, tm, tn):
    t, wdt = ya.shape
    d = w_branch.shape[2]
    nb = d // tn
    y_spec = pl.BlockSpec((tm, wdt), lambda i, j: (i, 0))
    g_spec = lambda n: pl.BlockSpec((tm, tn), lambda i, j: (i, gate_col0 + n * nb + j))
    return pl.pallas_call(
        _merge_kernel,
        grid=(t // tm, nb),
        in_specs=[y_spec, y_spec, y_spec, g_spec(0), g_spec(1), g_spec(2),
                  pl.BlockSpec((N_BRANCH, wdt, tn), lambda i, j: (0, 0, j))],
        out_specs=pl.BlockSpec((tm, tn), lambda i, j: (i, j)),
        out_shape=jax.ShapeDtypeStruct((t, d), BF16),
        compiler_params=_cparams(("parallel", "arbitrary")),
        name="merge",
    )(ya, yb, yc, proj, proj, proj, w_branch)


def _outproj_kernel(m_ref, x_ref, g1_ref, w_ref, o_ref):
    o_ref[...] = x_ref[...] + g1_ref[...] * jnp.dot(m_ref[...], w_ref[...].astype(BF16),
                                                   preferred_element_type=F32)


def _outproj_call(merged, x, grp, layer, gate_chunk, w, tn):
    t, d = x.shape
    tm = grp.tm
    nb = d // tn
    if grp.per_token:
        g1 = pl.BlockSpec((None, tm, tn), lambda i, j: (layer, i, gate_chunk * nb + j))
    else:
        seq = grp.seq
        g1 = pl.BlockSpec((None, None, 1, tn), lambda i, j: (layer, (i * tm) // seq, 0, gate_chunk * nb + j))
    return pl.pallas_call(
        _outproj_kernel,
        grid=(t // tm, nb),
        in_specs=[pl.BlockSpec((tm, d), lambda i, j: (i, 0)),
                  pl.BlockSpec((tm, tn), lambda i, j: (i, j)),
                  g1,
                  pl.BlockSpec((d, tn), lambda i, j: (0, j))],
        out_specs=pl.BlockSpec((tm, tn), lambda i, j: (i, j)),
        out_shape=jax.ShapeDtypeStruct((t, d), F32),
        compiler_params=_cparams(("parallel", "arbitrary")),
        name="outproj",
    )(merged, x, grp.mod_arr, w)


def _first_max(slabs, rowi, n_rows):
    m = functools.reduce(jnp.maximum, [jnp.max(s, axis=0, keepdims=True) for s in slabs])
    cand = [jnp.min(jnp.where(s == m, rowi + g * SUBLANE, n_rows), axis=0, keepdims=True)
            for g, s in enumerate(slabs)]
    return m, functools.reduce(jnp.minimum, cand)


def _router_kernel(x_ref, sc_ref, sh_ref, g_ref, wrt_ref, rb_ref, tri_ref, wsg_ref, wsu_ref, wsd_ref,
                   h_ref, s_ref, idx_ref, w_ref, rank_ref, cnt_ref, *, n_experts):
    h = _rms(x_ref[...]) * g_ref[...]
    h = h * (1.0 + sc_ref[...]) + sh_ref[...]
    h_ref[...] = h.reshape(h_ref.shape)
    hb = h.astype(BF16)
    a = _silu(jnp.dot(hb, wsg_ref[...].astype(BF16), preferred_element_type=F32))
    a = a * jnp.dot(hb, wsu_ref[...].astype(BF16), preferred_element_type=F32)
    s_ref[...] = _dot(a, wsd_ref[...])
    tm = h.shape[0]
    ge = n_experts // N_GROUPS
    scores = _sigmoid(_dot_nt(wrt_ref[...], h))
    sel = scores + rb_ref[...]
    neg = -jnp.inf
    rowi = lax.broadcasted_iota(jnp.int32, (ge, tm), 0)
    slabs = [sel[g * ge:(g + 1) * ge] for g in range(N_GROUPS)]
    sc_slabs = [scores[g * ge:(g + 1) * ge] for g in range(N_GROUPS)]

    gsc = jnp.zeros((N_GROUPS, tm), F32)
    for g, slab in enumerate(slabs):
        m1, i1 = _first_max([slab], rowi, ge)
        m2 = jnp.max(jnp.where(rowi == i1, neg, slab), axis=0, keepdims=True)
        gsc = jnp.where(rowi == g, m1 + m2, gsc)
    gkeep = jnp.zeros((N_GROUPS, tm), F32)
    for _ in range(TOPK_GROUPS):
        _, gi = _first_max([gsc], rowi, N_GROUPS)
        hit = rowi == gi
        gkeep = jnp.where(hit, 1.0, gkeep)
        gsc = jnp.where(hit, neg, gsc)
    slabs = [jnp.where(gkeep[g:g + 1] > 0.0, slab, neg) for g, slab in enumerate(slabs)]

    member = [jnp.zeros((ge, tm), F32) for _ in range(N_GROUPS)]
    idx_rows, w_rows = [], []
    for _ in range(TOP_K):
        _, ei = _first_max(slabs, rowi, n_experts)
        wk = jnp.zeros((1, tm), F32)
        for g in range(N_GROUPS):
            hit = (rowi + g * ge) == ei
            member[g] = jnp.where(hit, 1.0, member[g])
            slabs[g] = jnp.where(hit, neg, slabs[g])
            wk = wk + jnp.sum(jnp.where(hit, sc_slabs[g], 0.0), axis=0, keepdims=True)
        idx_rows.append(ei)
        w_rows.append(wk)
    w_sum = functools.reduce(lambda a, b: a + b, w_rows)

    memb = jnp.concatenate(member, axis=0)
    rank_full = jnp.dot(memb.astype(BF16), tri_ref[...], preferred_element_type=F32)
    cnt_ref[...] = jnp.broadcast_to(jnp.sum(memb, axis=-1, keepdims=True), cnt_ref.shape)
    rowk = lax.broadcasted_iota(jnp.int32, (TOP_K, tm), 0)
    idx_out = jnp.zeros((TOP_K, tm), jnp.int32)
    w_out = jnp.zeros((TOP_K, tm), F32)
    rank_out = jnp.zeros((TOP_K, tm), F32)
    for k in range(TOP_K):
        rk = jnp.zeros((1, tm), F32)
        for g in range(N_GROUPS):
            rk = rk + jnp.sum(jnp.where((rowi + g * ge) == idx_rows[k], rank_full[g * ge:(g + 1) * ge], 0.0),
                              axis=0, keepdims=True)
        idx_out = jnp.where(rowk == k, idx_rows[k], idx_out)
        w_out = jnp.where(rowk == k, w_rows[k] / w_sum * ROUTED_SCALE, w_out)
        rank_out = jnp.where(rowk == k, rk, rank_out)
    idx_ref[...] = idx_out
    w_ref[...] = w_out
    rank_ref[...] = rank_out.astype(jnp.int32)


def _router_call(x, grp, layer, chunks, g, w_router, router_bias, ws_gate, ws_up, ws_down):
    t, d = x.shape
    e = w_router.shape[1]
    ds = ws_gate.shape[1]
    tm = grp.tm
    assert e // N_GROUPS == SUBLANE
    tri = jnp.triu(jnp.ones((tm, tm), BF16), 1)
    kern = functools.partial(_router_kernel, n_experts=e)
    row = pl.BlockSpec((TOP_K, tm), lambda i: (0, i))
    full = lambda a, b: pl.BlockSpec((a, b), lambda i: (0, 0))
    return pl.pallas_call(
        kern,
        grid=(t // tm,),
        in_specs=[pl.BlockSpec((tm, d), lambda i: (i, 0)),
                  grp.mod_spec(layer, chunks[0], d),
                  grp.mod_spec(layer, chunks[1], d),
                  full(1, d), full(e, d), full(e, 1), full(tm, tm), full(d, ds), full(d, ds), full(ds, d)],
        out_specs=[pl.BlockSpec((tm, d // LANE, LANE), lambda i: (i, 0, 0)),
                   pl.BlockSpec((tm, d), lambda i: (i, 0)), row, row, row,
                   pl.BlockSpec((None, e, LANE), lambda i: (i, 0, 0))],
        out_shape=[jax.ShapeDtypeStruct((t, d // LANE, LANE), F32), jax.ShapeDtypeStruct((t, d), F32),
                   jax.ShapeDtypeStruct((TOP_K, t), jnp.int32), jax.ShapeDtypeStruct((TOP_K, t), F32),
                   jax.ShapeDtypeStruct((TOP_K, t), jnp.int32), jax.ShapeDtypeStruct((t // tm, e, LANE), F32)],
        compiler_params=_cparams(("parallel",)),
        name="router",
    )(x, grp.mod_arr, grp.mod_arr, g.reshape(1, d), jnp.transpose(w_router), router_bias.reshape(e, 1).astype(F32),
      tri, ws_gate, ws_up, ws_down)


def _dispatch_kernel(dest_ref, h_ref, xs_hbm, sem, *, tc):
    def issue(r, carry):
        pltpu.make_async_copy(h_ref.at[lax.rem(r, tc)], xs_hbm.at[dest_ref[r]], sem).start()
        return carry

    lax.fori_loop(0, tc * TOP_K, issue, 0, unroll=8)
    for _ in range(TOP_K):
        pltpu.make_async_copy(h_ref, xs_hbm.at[pl.ds(0, tc)], sem).wait()


def _dispatch_call(dest_km, h3, tc):
    t, s, lanes = h3.shape
    kern = functools.partial(_dispatch_kernel, tc=tc)
    return pl.pallas_call(
        kern,
        grid=(t // tc,),
        in_specs=[pl.BlockSpec((None, None, TOP_K * tc), lambda i: (i, 0, 0), memory_space=pltpu.SMEM),
                  pl.BlockSpec((tc, s, lanes), lambda i: (i, 0, 0))],
        out_specs=pl.BlockSpec(memory_space=pl.ANY),
        out_shape=jax.ShapeDtypeStruct((t * TOP_K, s, lanes), F32),
        scratch_shapes=[pltpu.SemaphoreType.DMA(())],
        compiler_params=_cparams(("arbitrary",)),
        name="dispatch",
    )(dest_km, h3)


def _experts_kernel(iblk_ref, ie_ref, ilo_ref, ihi_ref, ilast_ref, x_ref, wg_ref, wu_ref, wd_ref, y_ref,
                    wg_scr, wu_scr, wd_scr, y_scr, *, blk):
    i = pl.program_id(0)
    lo = ilo_ref[i]
    hi = ihi_ref[i]
    prev = ie_ref[jnp.maximum(i - 1, 0)]
    d = y_scr.shape[1]

    @pl.when((i == 0) | (prev != ie_ref[i]))
    def _():
        wg_scr[...] = wg_ref[...].astype(BF16)
        wu_scr[...] = wu_ref[...].astype(BF16)
        wd_scr[...] = wd_ref[...].astype(BF16)

    @pl.when(hi > lo)
    def _():
        xb = x_ref[...].reshape(blk, d).astype(BF16)
        a = _silu(jnp.dot(xb, wg_scr[...], preferred_element_type=F32))
        a = a * jnp.dot(xb, wu_scr[...], preferred_element_type=F32)
        res = jnp.dot(a.astype(BF16), wd_scr[...], preferred_element_type=F32)
        row = lax.broadcasted_iota(jnp.int32, (blk, 1), 0)
        mine = (row >= lo) & (row < hi)

        @pl.when(lo == 0)
        def _():
            y_scr[...] = jnp.where(mine, res, 0.0)

        @pl.when(lo > 0)
        def _():
            y_scr[...] = jnp.where(mine, res, y_scr[...])

    @pl.when(ilast_ref[i] == 1)
    def _():
        y_ref[...] = y_scr[...].reshape(y_ref.shape)


def _experts_call(items, xs, w_gate, w_up, w_down, layer, blk):
    iblk, ie, ilo, ihi, ilast = items
    n_rows, s, lanes = xs.shape
    d = s * lanes
    de = w_gate.shape[3]
    kern = functools.partial(_experts_kernel, blk=blk)
    wspec = lambda a, b: pl.BlockSpec((None, None, a, b), lambda i, ib, ie_, *_: (layer, ie_[i], 0, 0))
    rows = pl.BlockSpec((blk, s, lanes), lambda i, ib, *_: (ib[i], 0, 0))
    grid_spec = pltpu.PrefetchScalarGridSpec(
        num_scalar_prefetch=5,
        grid=(iblk.shape[0],),
        in_specs=[rows, wspec(d, de), wspec(d, de), wspec(de, d)],
        out_specs=rows,
        scratch_shapes=[pltpu.VMEM((d, de), BF16), pltpu.VMEM((d, de), BF16), pltpu.VMEM((de, d), BF16),
                        pltpu.VMEM((blk, d), F32)],
    )
    return pl.pallas_call(
        kern,
        grid_spec=grid_spec,
        out_shape=jax.ShapeDtypeStruct((n_rows, s, lanes), F32),
        compiler_params=_cparams(("arbitrary",)),
        name="experts",
    )(iblk, ie, ilo, ihi, ilast, xs, w_gate, w_up, w_down)


def _combine_kernel(pos_ref, x_ref, sh_ref, g2_ref, fg_ref, w_ref, y_hbm, o_ref, buf, sem, *, tc, final):
    def issue(r, carry):
        pltpu.make_async_copy(y_hbm.at[pos_ref[r]], buf.at[r], sem).start()
        return carry

    lax.fori_loop(0, tc * TOP_K, issue, 0, unroll=8)
    pltpu.make_async_copy(y_hbm.at[pl.ds(0, tc * TOP_K)], buf, sem).wait()
    acc = buf[pl.ds(0, tc)] * w_ref[:, 0:1, :]
    for k in range(1, TOP_K):
        acc = acc + buf[pl.ds(k * tc, tc)] * w_ref[:, k:k + 1, :]
    out = x_ref[...] + g2_ref[...] * (acc.reshape(x_ref.shape) + sh_ref[...])
    o_ref[...] = _rms(out) * fg_ref[...] if final else out


def _combine_call(pos_km, wts, x, shared, grp, layer, gate_chunk, final_g, y_sorted, tc, final):
    t, d = x.shape
    kern = functools.partial(_combine_kernel, tc=tc, final=final)
    seq = grp.seq
    if grp.per_token:
        g2 = pl.BlockSpec((None, tc, d), lambda i: (layer, i, gate_chunk))
    else:
        g2 = pl.BlockSpec((None, None, 1, d), lambda i: (layer, (i * tc) // seq, 0, gate_chunk))
    tile = pl.BlockSpec((tc, d), lambda i: (i, 0))
    return pl.pallas_call(
        kern,
        grid=(t // tc,),
        in_specs=[pl.BlockSpec((None, None, TOP_K * tc), lambda i: (i, 0, 0), memory_space=pltpu.SMEM),
                  tile, tile, g2,
                  pl.BlockSpec((1, d), lambda i: (0, 0)),
                  pl.BlockSpec((tc, TOP_K, LANE), lambda i: (i, 0, 0)),
                  pl.BlockSpec(memory_space=pl.ANY)],
        out_specs=tile,
        out_shape=jax.ShapeDtypeStruct((t, d), F32),
        scratch_shapes=[pltpu.VMEM((TOP_K * tc,) + y_sorted.shape[1:], F32), pltpu.SemaphoreType.DMA(())],
        compiler_params=_cparams(("arbitrary",)),
        name="combine",
    )(pos_km, x, shared, grp.mod_arr, final_g.reshape(1, d), wts, y_sorted)


def _route_plan(idx_t, rank_t, counts, tm, blk, tc):
    _, t = idx_t.shape
    n_tiles, n_experts = counts.shape
    n_rows = t * TOP_K
    total = jnp.sum(counts, axis=0)
    e_end = jnp.cumsum(total)
    e_start = e_end - total
    table = e_start[None, :] + jnp.cumsum(counts, axis=0) - counts
    idx4 = idx_t.reshape(TOP_K, n_tiles, tm, 1)
    hit = idx4 == jnp.arange(n_experts, dtype=jnp.int32)
    dest = jnp.sum(jnp.where(hit, table[None, :, None, :], 0), axis=-1).reshape(TOP_K, t) + rank_t
    dest_km = dest.reshape(TOP_K, t // tc, tc).transpose(1, 0, 2).reshape(t // tc, 1, TOP_K * tc)

    n_blocks = n_rows // blk
    b_cut = jnp.arange(n_blocks, dtype=jnp.int32) * blk
    e_cut = e_start[1:].astype(jnp.int32)
    n_items = n_blocks + n_experts - 1
    b_at = (jnp.arange(n_blocks, dtype=jnp.int32)
            + jnp.sum(e_cut[None, :] < b_cut[:, None], axis=1).astype(jnp.int32))
    e_at = (jnp.arange(n_experts - 1, dtype=jnp.int32)
            + jnp.minimum(e_cut // blk + 1, n_blocks).astype(jnp.int32))
    cuts = jnp.zeros((n_items,), jnp.int32).at[b_at].set(b_cut).at[e_at].set(e_cut)
    ends = jnp.concatenate([cuts[1:], jnp.array([n_rows], jnp.int32)])
    iblk = jnp.minimum(cuts // blk, n_blocks - 1).astype(jnp.int32)
    ie = jnp.minimum(jnp.sum(e_end[None, :] <= cuts[:, None], axis=1), n_experts - 1).astype(jnp.int32)
    ilo = (cuts - iblk * blk).astype(jnp.int32)
    ihi = (ends - iblk * blk).astype(jnp.int32)
    ilast = jnp.concatenate([iblk[1:] != iblk[:-1], jnp.array([True])]).astype(jnp.int32)
    return dest_km, (iblk, ie, ilo, ihi, ilast)


def _bucket_bias(dist, rel_bias):
    n = jnp.maximum(dist, 0)
    exact = N_BUCKETS // 2
    nf = jnp.maximum(n, exact).astype(F32)
    large = exact + (jnp.log(nf / exact) / math.log(MAX_DISTANCE / exact) * (N_BUCKETS - exact)).astype(jnp.int32)
    bucket = jnp.where(n < exact, n, jnp.minimum(large, N_BUCKETS - 1))
    return rel_bias[bucket].astype(F32)


def _prompt_bias_table(rel_bias, tq, tk):
    ratio = tq // tk
    assert tk + 1 >= MAX_DISTANCE
    n_heads = rel_bias.shape[1]
    length = tq + tk
    tabs = []
    for b in range(ratio + 2):
        d0 = (b - (ratio - 1)) * tk
        dist = d0 - (tk - 1) + jnp.arange(length, dtype=jnp.int32)
        f = jnp.where((dist >= 0)[:, None], _bucket_bias(dist, rel_bias), -jnp.inf)
        g = jnp.transpose(f)
        flat = jnp.tile(g, (1, tq + 1))[:, :tq * (length + 1)]
        hankel = flat.reshape(n_heads, tq, length + 1)[:, :, :tk]
        tabs.append(hankel[:, :, ::-1])
    return jnp.stack(tabs, axis=1)


def _sample_bias_tables(rel_bias, steps, past_len, page):
    assert page + 1 >= MAX_DISTANCE
    qpos = past_len + jnp.arange(steps, dtype=jnp.int32)
    nrow = C_HEADS * 2 * steps
    tabs = []
    for first in (past_len - 2 * page, past_len - page):
        kpos = first + jnp.arange(page, dtype=jnp.int32)
        b = _bucket_bias(qpos[:, None] - kpos[None, :], rel_bias)
        b = jnp.transpose(b, (2, 0, 1))
        b = jnp.broadcast_to(b[:, None], (C_HEADS, 2, steps, page)).reshape(nrow, page)
        own = (jnp.arange(nrow)[:, None] // (2 * steps)) == jnp.arange(C_HEADS)[None, :]
        tabs.append(jnp.where(own[:, None, :], b[:, :, None], -jnp.inf).reshape(nrow, page * C_HEADS))
    bp = jnp.stack(tabs)
    dn = qpos[:, None] - qpos[None, :]
    bn = jnp.where((dn >= 0)[..., None], _bucket_bias(dn, rel_bias), -jnp.inf)
    bn = jnp.transpose(bn, (2, 0, 1))
    bn = jnp.broadcast_to(bn[:, None], (C_HEADS, 2, steps, steps)).reshape(C_HEADS * 2 * steps, steps)
    return bp, bn


def _trunk(x3, grp, p, past, depth):
    bsz, seq, d = x3.shape
    t = bsz * seq
    is_sample = past is not None
    x = x3.reshape(t, d)
    hw = B_HEADS * B_DK
    wdt = A_GROUPS * LANE
    tm = grp.tm
    blk = 256
    tc = min(128, t)
    new_k, new_v, new_s, new_conv, new_va = [], [], [], [], []
    for l in range(depth):
        proj = _inproj_call(x, grp, l, (1, 0), p['norm_mix_g'][l], p['w_in_main'][l], p['tn_in'])
        proj_ab = _inproj_call(x, grp, l, (1, 0), p['norm_mix_g'][l], p['w_in_ab'][l], LANE)
        proj3 = proj.reshape(bsz, seq, -1)

        c_a = min(seq, A_CHUNK)
        causal = jnp.tril(jnp.ones((c_a, c_a), bool))
        w_s = jnp.where(causal[None], p['spatial_w'][l][:, :c_a, :c_a], 0.0)
        sb = p['spatial_b'][l][:, :c_a]
        if c_a < A_CHUNK:
            rep = A_CHUNK // c_a
            w_s = jax.vmap(lambda m: jnp.kron(jnp.eye(rep, dtype=F32), m))(w_s)
            sb = jnp.tile(sb, (1, rep))
        sb_full = jnp.repeat(jnp.transpose(sb), LANE, axis=1)
        y_a, v_norm = _mixa_call(proj, w_s, sb_full, p['ln_v_g'][l], p['ln_v_b'][l])

        a_b = proj_ab[:, :B_HEADS]
        b_b = proj_ab[:, B_HEADS:2 * B_HEADS]
        beta = jax.nn.sigmoid(b_b).reshape(bsz, seq, B_HEADS)
        g_log = (-jnp.exp(p['A_log'][l].astype(F32))
                 * jax.nn.softplus(a_b + p['dt_bias'][l].astype(F32))).reshape(bsz, seq, B_HEADS)
        qkv_raw = proj3[:, :, 2 * wdt:2 * wdt + 3 * hw]
        if is_sample:
            tr = lambda a: jnp.transpose(a, (1, 0, 2))
            y_b_t, s_new = _gdn_sample_call(tr(qkv_raw), tr(past['state_conv'][l]),
                                            tr(proj3[:, :, 2 * wdt + 3 * hw:2 * wdt + 4 * hw]),
                                            tr(g_log), tr(beta), past['state_gdn'][l], p['conv_w'][l],
                                            p['gdn_norm_g'][l])
            y_b = tr(y_b_t).reshape(t, hw).astype(BF16)
            conv_state = jnp.concatenate([past['state_conv'][l], qkv_raw], axis=1)[:, seq:]
        else:
            c_g = math.gcd(seq, GDN_CHUNK)
            gc_row = jnp.cumsum(jnp.transpose(g_log.reshape(bsz, seq // c_g, c_g, B_HEADS), (0, 1, 3, 2)), axis=-1)
            gc_tok = jnp.transpose(gc_row, (0, 1, 3, 2)).reshape(bsz, seq, B_HEADS)
            y_b3, s_new = _gdn_prompt_call(proj3, 2 * wdt // hw, p['conv_w'][l], gc_tok, beta, gc_row,
                                           p['gdn_norm_g'][l])
            y_b = y_b3.reshape(t, hw)
            conv_state = jnp.concatenate([jnp.zeros((bsz, CONV_W - 1, 3 * hw), F32), qkv_raw], axis=1)[:, seq:]

        c0 = 2 * wdt + 4 * hw
        cw = C_HEADS * 2 * C_DQ
        k_c = proj3[:, :, c0 + cw:c0 + 2 * cw]
        v_c = proj3[:, :, c0 + 2 * cw:c0 + 3 * cw]
        lam_init = 0.8 - 0.6 * math.exp(-0.3 * l)
        lam = (jnp.exp(jnp.sum(p['lambda_q1'][l].astype(F32) * p['lambda_k1'][l].astype(F32)))
               - jnp.exp(jnp.sum(p['lambda_q2'][l].astype(F32) * p['lambda_k2'][l].astype(F32))) + lam_init)
        lam = lam.reshape(1).astype(F32)
        if is_sample:
            y_c = _attn_sample_call(past['page_table'], lam, proj3[:, :, c0:c0 + cw], k_c, v_c,
                                    p['bias_new'], p['bias_past'], p['subln_g'][l], past['cache_k'],
                                    past['cache_v'], l, 1.0 - lam_init, p['ppg'])
            y_c = y_c.reshape(t, cw).astype(BF16)
        else:
            nb = c0 // (2 * C_DQ)
            y_c = _attn_prompt_call(proj3, nb, nb + C_HEADS, nb + 2 * C_HEADS, p['bias_prompt'], lam,
                                    p['subln_g'][l], 1.0 - lam_init, p['tq']).reshape(t, cw)

        gate0 = c0 + 3 * cw
        tn_m = 512
        merged = _merge_call(y_a, y_b, y_c, proj, gate0 // tn_m, p['w_branch'][l], tm, tn_m)
        x = _outproj_call(merged, x, grp, l, 2, p['w_out'][l], 512)

        h3, shared, idx_t, w_t, rank_t, cnt = _router_call(
            x, grp, l, (4, 3), p['norm_moe_g'][l], p['w_router'][l], p['router_bias'][l],
            p['ws_gate'][l], p['ws_up'][l], p['ws_down'][l])
        dest_km, items = _route_plan(idx_t, rank_t, cnt[:, :, 0].astype(jnp.int32), tm, blk, tc)
        xs = _dispatch_call(dest_km, h3, tc)
        y_sorted = _experts_call(items, xs, p['w_gate'], p['w_up'], p['w_down'], l, blk)
        w_rows = jnp.broadcast_to(jnp.transpose(w_t)[:, :, None], (t, TOP_K, LANE))
        x = _combine_call(dest_km, w_rows, x, shared, grp, l, 5, p['final_norm_g'], y_sorted, tc,
                          l == depth - 1)

        k_heads, v_heads = _heads_call(proj, (c0 + cw) // cw, (c0 + 2 * cw) // cw, C_HEADS)
        new_k.append(k_heads.reshape(bsz, seq, C_HEADS, 2 * C_DQ))
        new_v.append(v_heads.reshape(bsz, seq, C_HEADS, 2 * C_DQ))
        new_s.append(s_new)
        new_conv.append(conv_state)
        if is_sample:
            new_va.append(v_norm.reshape(bsz, seq, wdt))
    chunk_v = jnp.stack(new_va) if is_sample else None
    return (x.reshape(bsz, seq, d), jnp.stack(new_k), jnp.stack(new_v), jnp.stack(new_s), jnp.stack(new_conv),
            chunk_v)


def kernel(x_prompt, x_sample, cache_k, cache_v, state_gdn, state_conv, page_table, c_prompt, c_sample,
           w_mod, b_mod, norm_mix_g, w_in, spatial_w, spatial_b, ln_v_g, ln_v_b, conv_w, A_log, dt_bias,
           gdn_norm_g, lambda_q1, lambda_k1, lambda_q2, lambda_k2, subln_g, rel_bias, w_branch, w_out,
           norm_moe_g, w_router, router_bias, w_gate, w_up, w_down, ws_gate, ws_up, ws_down, final_norm_g):
    depth = w_mod.shape[0]
    bsz, seq, d = x_prompt.shape
    dbsz, dseq, _ = x_sample.shape
    wdt = A_GROUPS * LANE
    hw = B_HEADS * B_DK
    n_ab = 2 * B_HEADS
    ab0 = 2 * wdt + 4 * hw
    w_in_main = jnp.concatenate([w_in[:, :, :ab0], w_in[:, :, ab0 + n_ab:]], axis=2)
    w_in_ab = jnp.pad(w_in[:, :, ab0:ab0 + n_ab], ((0, 0), (0, 0), (0, LANE - n_ab)))
    n_main = w_in_main.shape[2]
    tn_in = next(c for c in (1536, 1280, 1024, 512, 256, 128) if n_main % c == 0)

    mod = _mod_call(jnp.concatenate([c_prompt, c_sample], axis=0), w_mod, b_mod)
    mod_prompt = mod[:, :bsz].reshape(depth, bsz, 1, mod.shape[2])
    mod_sample = jnp.repeat(mod[:, bsz:], dseq, axis=1)

    past_len = page_table.shape[1] * cache_k.shape[2]
    tq = min(512, seq)
    bias_past, bias_new = _sample_bias_tables(rel_bias, dseq, past_len, cache_k.shape[2])
    p = dict(norm_mix_g=norm_mix_g, w_in_main=w_in_main, w_in_ab=w_in_ab, tn_in=tn_in, spatial_w=spatial_w,
             spatial_b=spatial_b, ln_v_g=ln_v_g, ln_v_b=ln_v_b, conv_w=conv_w, A_log=A_log, dt_bias=dt_bias,
             gdn_norm_g=gdn_norm_g, lambda_q1=lambda_q1, lambda_k1=lambda_k1, lambda_q2=lambda_q2,
             lambda_k2=lambda_k2, subln_g=subln_g, w_branch=w_branch, w_out=w_out, norm_moe_g=norm_moe_g,
             w_router=w_router, router_bias=router_bias, w_gate=w_gate, w_up=w_up, w_down=w_down,
             ws_gate=ws_gate, ws_up=ws_up, ws_down=ws_down, final_norm_g=final_norm_g,
             bias_prompt=_prompt_bias_table(rel_bias, tq, min(LANE, seq)), tq=tq, bias_past=bias_past, bias_new=bias_new,
             ppg=math.gcd(page_table.shape[1], 8))
    past = dict(cache_k=cache_k, cache_v=cache_v,
                state_gdn=state_gdn, state_conv=state_conv, page_table=page_table)

    grp_p = _Group(mod_prompt, False, seq, min(512, seq))
    grp_s = _Group(mod_sample, True, dseq, min(512, dbsz * dseq))
    y_p, k_p, v_p, s_p, conv_p, _ = _trunk(x_prompt, grp_p, p, None, depth)
    y_s, k_s, v_s, s_s, conv_s, chunk_v = _trunk(x_sample, grp_s, p, past, depth)
    return (y_p, y_s, k_p, v_p, k_s, v_s, s_p, s_s, conv_p, conv_s, chunk_v)
```
